```python
import math
import jax
import jax.numpy as jnp
from jax import lax
import numpy as np

D_MODEL = 1024
BATCH = 8
SEQ = 2048
DEPTH = 4

CHUNK = 64
NORM_EPS = 1e-6

HG_HEADS = 4
HG_DK = 128
HG_DV = 128
HG_KEY = HG_HEADS * HG_DK
HG_VAL = HG_HEADS * HG_DV

GLA_HEADS = 4
GLA_DK = 64
GLA_DV = 128
GLA_KEY = GLA_HEADS * GLA_DK
GLA_VAL = GLA_HEADS * GLA_DV
GLA_RANK = 16
GLA_TAU = 16.0

ML_HEADS = 4
ML_DK = 64
ML_DV = 128
ML_KEY = ML_HEADS * ML_DK
ML_VAL = ML_HEADS * ML_DV
CONV_K = 4

D_MIX = HG_VAL + GLA_VAL + ML_VAL

SEGMENTS = (
    ("hg_q", HG_KEY), ("hg_f", HG_KEY), ("hg_i", HG_VAL), ("hg_z", HG_VAL),
    ("gla_q", GLA_KEY), ("gla_k", GLA_KEY), ("gla_v", GLA_VAL), ("gla_a", GLA_RANK), ("gla_z", GLA_VAL),
    ("ml_q", ML_KEY), ("ml_k", ML_KEY), ("ml_v", ML_VAL), ("ml_i", ML_HEADS), ("ml_f", ML_HEADS),
    ("ml_o", ML_VAL), ("ml_z", ML_VAL),
)
N_IN = (3 * HG_KEY + HG_VAL) - HG_KEY + HG_VAL + (2 * GLA_KEY + 2 * GLA_VAL + GLA_RANK) + (2 * ML_KEY + 3 * ML_VAL + 2 * ML_HEADS)

kernel_name = "hymba_style_hgrn2_gla_mlstm_trunk"


def _segment_offsets():
    offs = {}
    start = 0
    for name, width in SEGMENTS:
        offs[name] = (start, start + width)
        start += width
    return offs


def _split_columns(proj):
    return {name: proj[..., s:e] for name, (s, e) in _segment_offsets().items()}


def _rmsnorm(x, w):
    xf = x.astype(jnp.float32)
    y = xf * lax.rsqrt(jnp.mean(xf * xf, axis=-1, keepdims=True) + NORM_EPS)
    return (y * w.astype(jnp.float32)).astype(x.dtype)


def _heads(a, n_heads):
    b, t, _ = a.shape
    return a.reshape(b, t, n_heads, -1).transpose(0, 2, 1, 3).astype(jnp.float32)


def _to_chunks(a, n_chunks):
    a = a.reshape(a.shape[:2] + (n_chunks, CHUNK) + a.shape[3:])
    return jnp.moveaxis(a, 2, 0)


def _from_chunks(a):
    a = jnp.moveaxis(a, 0, 2)
    return a.reshape(a.shape[:2] + (a.shape[2] * a.shape[3],) + a.shape[4:])


def _causal_conv(a, w, bias):
    k = w.shape[0]
    y = lax.conv_general_dilated(a, w[:, None, :].astype(a.dtype), window_strides=(1,), padding=[(k - 1, 0)],
                                 dimension_numbers=("NWC", "WIO", "NWC"), feature_group_count=a.shape[-1])
    return y + bias.astype(a.dtype)


def _chunk_gated_linear_attention(q, k, v, log_g):
    bsz, nh, seq, dk = q.shape
    dv = v.shape[-1]
    nc = seq // CHUNK
    causal = jnp.tril(jnp.ones((CHUNK, CHUNK), dtype=bool))[None, None, :, :, None]

    def step(state, inp):
        qc, kc, vc, gc = inp
        b = jnp.cumsum(gc, axis=2)
        dec = jnp.exp(jnp.where(causal, b[:, :, :, None, :] - b[:, :, None, :, :], -jnp.inf))
        att = jnp.einsum("bhtd,bhsd,bhtsd->bhts", qc, kc, dec)
        out = jnp.einsum("bhts,bhsv->bhtv", att, vc) + jnp.einsum("bhtd,bhdv->bhtv", qc * jnp.exp(b), state)
        b_last = b[:, :, -1:, :]
        state = jnp.exp(b_last[:, :, 0, :, None]) * state + jnp.einsum("bhsd,bhsv->bhdv", kc * jnp.exp(b_last - b), vc)
        return state, out

    s0 = jnp.zeros((bsz, nh, dk, dv), jnp.float32)
    _, out = lax.scan(step, s0, (_to_chunks(q, nc), _to_chunks(k, nc), _to_chunks(v, nc), _to_chunks(log_g, nc)))
    return _from_chunks(out)


def _chunk_mlstm(q, k, v, i_pre, log_f):
    bsz, nh, seq, dk = q.shape
    dv = v.shape[-1]
    nc = seq // CHUNK
    causal = jnp.tril(jnp.ones((CHUNK, CHUNK), dtype=bool))[None, None]

    def step(carry, inp):
        c_st, n_st, m_st = carry
        qc, kc, vc, ic, fc = inp
        b = jnp.cumsum(fc, axis=-1)
        d = jnp.where(causal, b[..., :, None] - b[..., None, :] + ic[..., None, :], -jnp.inf)
        m_t = jnp.maximum(b + m_st[..., None], jnp.max(d, axis=-1))
        w = jnp.exp(d - m_t[..., None])
        inter = jnp.exp(b + m_st[..., None] - m_t)
        s = jnp.einsum("bhtd,bhsd->bhts", qc, kc) * w
        num = inter[..., None] * jnp.einsum("bhtd,bhdv->bhtv", qc, c_st) + jnp.einsum("bhts,bhsv->bhtv", s, vc)
        den = inter * jnp.einsum("bhtd,bhd->bht", qc, n_st) + jnp.sum(s, axis=-1)
        h = num / jnp.maximum(jnp.abs(den), jnp.exp(-m_t))[..., None]
        m_new = m_t[..., -1]
        wk = jnp.exp(b[..., -1:] - b + ic - m_new[..., None])
        scale = jnp.exp(b[..., -1] + m_st - m_new)
        c_st = scale[..., None, None] * c_st + jnp.einsum("bhs,bhsd,bhsv->bhdv", wk, kc, vc)
        n_st = scale[..., None] * n_st + jnp.einsum("bhs,bhsd->bhd", wk, kc)
        return (c_st, n_st, m_new), h

    carry0 = (jnp.zeros((bsz, nh, dk, dv), jnp.float32), jnp.zeros((bsz, nh, dk), jnp.float32),
              jnp.zeros((bsz, nh), jnp.float32))
    _, h = lax.scan(step, carry0, (_to_chunks(q, nc), _to_chunks(k, nc), _to_chunks(v, nc),
                                   _to_chunks(i_pre, nc), _to_chunks(log_f, nc)))
    return _from_chunks(h)


def _gated_head_norm(o, z, w):
    bsz, nh, seq, dv = o.shape
    of = o.transpose(0, 2, 1, 3)
    of = of * lax.rsqrt(jnp.mean(of * of, axis=-1, keepdims=True) + NORM_EPS)
    of = of.reshape(bsz, seq, nh * dv) * w.astype(jnp.float32)
    return (of * jax.nn.silu(z.astype(jnp.float32))).astype(z.dtype)


def _hgrn2_branch(parts, lb):
    q = jax.nn.silu(_heads(parts["hg_q"], HG_HEADS))
    lb = lb.reshape(HG_HEADS, 1, HG_DK)
    f = lb + (1.0 - lb) * jax.nn.sigmoid(_heads(parts["hg_f"], HG_HEADS))
    i = _heads(parts["hg_i"], HG_HEADS)
    return _chunk_gated_linear_attention(q, 1.0 - f, i, jnp.log(f))


def _gla_branch(parts, w_a2, b_a2):
    q = _heads(parts["gla_q"], GLA_HEADS) * (GLA_DK ** -0.5)
    k = _heads(parts["gla_k"], GLA_HEADS)
    v = _heads(parts["gla_v"], GLA_HEADS)
    a = parts["gla_a"] @ w_a2 + b_a2
    log_alpha = jax.nn.log_sigmoid(_heads(a, GLA_HEADS)) / GLA_TAU
    return _chunk_gated_linear_attention(q, k, v, log_alpha)


def _mlstm_branch(parts, conv_w, conv_b):
    qk = jax.nn.silu(_causal_conv(jnp.concatenate([parts["ml_q"], parts["ml_k"]], axis=-1), conv_w, conv_b))
    q = _heads(qk[..., :ML_KEY], ML_HEADS) * (ML_DK ** -0.5)
    k = _heads(qk[..., ML_KEY:], ML_HEADS)
    v = _heads(parts["ml_v"], ML_HEADS)
    i_pre = parts["ml_i"].astype(jnp.float32).transpose(0, 2, 1)
    log_f = jax.nn.log_sigmoid(parts["ml_f"].astype(jnp.float32).transpose(0, 2, 1))
    h = _chunk_mlstm(q, k, v, i_pre, log_f)
    return h * jax.nn.sigmoid(_heads(parts["ml_o"], ML_HEADS))


def setup_inputs(seed: int = 0) -> dict:
    key = jax.random.key(seed)
    ks = jax.random.split(key, 16)
    nrm = jax.random.normal
    x = nrm(ks[0], (BATCH, SEQ, D_MODEL), jnp.float32)
    norm_w = 1.0 + 0.02 * nrm(ks[1], (DEPTH, D_MODEL), jnp.float32)
    w_in = nrm(ks[2], (DEPTH, D_MODEL, N_IN), jnp.float32) * (D_MODEL ** -0.5)
    b_in = 0.02 * nrm(ks[3], (DEPTH, N_IN), jnp.float32)
    f_start, f_end = _segment_offsets()["ml_f"]
    b_in = b_in.at[:, f_start:f_end].add(jnp.linspace(3.0, 6.0, ML_HEADS, dtype=jnp.float32))
    hg_lb_logits = 0.1 * nrm(ks[4], (DEPTH, HG_KEY), jnp.float32)
    hg_norm_w = 1.0 + 0.02 * nrm(ks[5], (DEPTH, HG_VAL), jnp.float32)
    gla_w_a2 = nrm(ks[6], (DEPTH, GLA_RANK, GLA_KEY), jnp.float32) * (GLA_RANK ** -0.5)
    gla_b_a2 = 0.02 * nrm(ks[7], (DEPTH, GLA_KEY), jnp.float32)
    gla_norm_w = 1.0 + 0.02 * nrm(ks[8], (DEPTH, GLA_VAL), jnp.float32)
    ml_conv_w = nrm(ks[9], (DEPTH, CONV_K, 2 * ML_KEY), jnp.float32) * (CONV_K ** -0.5)
    ml_conv_b = 0.02 * nrm(ks[10], (DEPTH, 2 * ML_KEY), jnp.float32)
    ml_norm_w = 1.0 + 0.02 * nrm(ks[11], (DEPTH, ML_VAL), jnp.float32)
    w_out = nrm(ks[12], (DEPTH, D_MIX, D_MODEL), jnp.float32) * (D_MIX ** -0.5) * 0.5
    final_norm_w = 1.0 + 0.02 * nrm(ks[13], (D_MODEL,), jnp.float32)
    return {"x": x, "norm_w": norm_w, "w_in": w_in, "b_in": b_in, "hg_lb_logits": hg_lb_logits,
            "hg_norm_w": hg_norm_w, "gla_w_a2": gla_w_a2, "gla_b_a2": gla_b_a2, "gla_norm_w": gla_norm_w,
            "ml_conv_w": ml_conv_w, "ml_conv_b": ml_conv_b, "ml_norm_w": ml_norm_w, "w_out": w_out,
            "final_norm_w": final_norm_w}


def reference(x, norm_w, w_in, b_in, hg_lb_logits, hg_norm_w, gla_w_a2, gla_b_a2, gla_norm_w,
              ml_conv_w, ml_conv_b, ml_norm_w, w_out, final_norm_w):
    p = jax.nn.softmax(hg_lb_logits.astype(jnp.float32), axis=0)
    lower_bounds = jnp.cumsum(p, axis=0) - p[0:1]
    for l in range(DEPTH):
        h = _rmsnorm(x, norm_w[l])
        parts = _split_columns(h @ w_in[l] + b_in[l])
        y_hg = _gated_head_norm(_hgrn2_branch(parts, lower_bounds[l]), parts["hg_z"], hg_norm_w[l])
        y_gla = _gated_head_norm(_gla_branch(parts, gla_w_a2[l], gla_b_a2[l]), parts["gla_z"], gla_norm_w[l])
        y_ml = _gated_head_norm(_mlstm_branch(parts, ml_conv_w[l], ml_conv_b[l]), parts["ml_z"], ml_norm_w[l])
        y = jnp.concatenate([y_hg, y_gla, y_ml], axis=-1).astype(x.dtype)
        x = x + y @ w_out[l]
    return _rmsnorm(x, final_norm_w)
```

```python
import functools

import numpy as np
import jax
import jax.numpy as jnp
from jax import lax
from jax.experimental import pallas as pl
from jax.experimental.pallas import tpu as pltpu

D_MODEL = 1024
CHUNK = 64
NORM_EPS = 1e-6
N_LEVELS = 6
GLA_TAU = 16.0
TIME_BLOCK = 256
VMEM_LIMIT_BYTES = 56 * 1024 * 1024

C_HGQ, C_HGF, C_HGI, C_HGZ = 0, 512, 1024, 1536
C_GQ, C_GK, C_GV, C_GZ = 2048, 2304, 2560, 3072
C_MQ, C_MK, C_MV, C_MO, C_MZ = 3584, 3840, 4096, 4608, 5120
C_SMALL = 5632
N_PROJ = 5760
S_GLA_A, S_ML_I, S_ML_F = 0, 16, 20
PROJ_GROUP = 512

F32 = jnp.float32
BF16 = jnp.bfloat16
NT = (((1,), (1,)), ((), ()))
TN = (((0,), (0,)), ((), ()))


def _dot(a, b, dims=None):
    if dims is None:
        return jnp.dot(a, b, preferred_element_type=F32)
    return lax.dot_general(a, b, dims, preferred_element_type=F32)


def _split_dot(a_bf16, x):
    hi = x.astype(BF16)
    lo = (x - hi.astype(F32)).astype(BF16)
    return _dot(a_bf16, hi) + _dot(a_bf16, lo)


def _sigmoid(x):
    return 1.0 / (1.0 + jnp.exp(-x))


def _silu(x):
    return x * _sigmoid(x)


def _log_sigmoid(x):
    return jnp.minimum(x, 0.0) - jnp.log(1.0 + jnp.exp(-jnp.abs(x)))


def _constants():
    c = CHUNK
    t = np.arange(c)
    blocks = [(t[None, :] <= t[:, None]), (t[None, :] > t[:, None])]
    qrow, krow, pairmask = [], [], []
    for lvl in range(1, N_LEVELS + 1):
        s = c >> lvl
        pos = t % (2 * s)
        ref = (t // (2 * s)) * 2 * s + s - 1
        lo = np.minimum(t, ref)[:, None]
        hi = np.maximum(t, ref)[:, None]
        blocks.append((t[None, :] > lo) & (t[None, :] <= hi))
        qrow.append(pos >= s)
        krow.append(pos < s)
        pairmask.append((t[:, None] // (2 * s)) == (t[None, :] // (2 * s)))
    e_mat = np.concatenate(blocks, axis=0).astype(np.float32)
    qk_rows = np.stack([np.repeat(np.stack(qrow)[:, :, None], 128, axis=2),
                        np.repeat(np.stack(krow)[:, :, None], 128, axis=2)]).astype(np.float32)
    pairmask.append(np.eye(c, dtype=bool))
    att_mask = np.tile(np.stack(pairmask).astype(np.float32), (1, 1, 4))
    bd_hg = np.kron(np.eye(2), np.ones((128, 128))).astype(np.float32)
    bd_g = np.kron(np.eye(4), np.ones((128, 64))).astype(np.float32)
    lane_head = np.kron(np.eye(4), np.ones((1, 64))).astype(np.float32)
    seg = np.zeros((128, 512), np.float32)
    for h in range(4):
        seg[S_ML_I + h, 64 * h:64 * (h + 1)] = 1.0
        seg[S_ML_F + h, 256 + 64 * h:256 + 64 * (h + 1)] = 1.0
    eye4 = np.tile(np.eye(c, dtype=np.float32), (1, 4))
    negm = np.tile(np.where(t[None, :] <= t[:, None], 0.0, -np.inf).astype(np.float32), (1, 4))
    return dict(e_mat=jnp.asarray(e_mat, BF16), qk_rows=jnp.asarray(qk_rows),
                att_mask=jnp.asarray(att_mask), bd_hg=jnp.asarray(bd_hg), bd_g=jnp.asarray(bd_g),
                lane_head=jnp.asarray(lane_head), seg=jnp.asarray(seg, BF16),
                eye4=jnp.asarray(eye4), negm=jnp.asarray(negm))


def _head_norm_gate(o, z, w):
    outs = []
    for h in range(4):
        oh = o[:, 128 * h:128 * (h + 1)]
        ms = jnp.mean(oh * oh, axis=-1, keepdims=True)
        outs.append(oh * lax.rsqrt(ms + NORM_EPS))
    return jnp.concatenate(outs, axis=1) * w * _silu(z)


def _layer_kernel(layer, apply_final_norm,
                  x_ref, nw_ref, win_ref, bin_ref, lbl_ref, hgnw_ref, wa2_ref, ba2_ref, glanw_ref,
                  convw_ref, convb_ref, mlnw_ref, wout_ref, fnw_ref,
                  e_ref, qkrows_ref, attmask_ref, bdhg_ref, bdg_ref, lanehead_ref, seg_ref, eye4_ref, negm_ref,
                  out_ref,
                  p_ref, xc_ref, y_ref, sthg_ref, stg_ref, ct_ref, nm_ref):
    tb = x_ref.shape[1]
    n_chunks = tb // CHUNK

    @pl.when(pl.program_id(1) == 0)
    def _():
        sthg_ref[...] = jnp.zeros_like(sthg_ref)
        stg_ref[...] = jnp.zeros_like(stg_ref)
        ct_ref[...] = jnp.zeros_like(ct_ref)
        nm_ref[...] = jnp.zeros_like(nm_ref)
        xc_ref[0:8, :] = jnp.zeros((8, 512), F32)

    x = x_ref[0]
    h = x * lax.rsqrt(jnp.mean(x * x, axis=-1, keepdims=True) + NORM_EPS) * nw_ref[...]
    hb = h.astype(BF16)
    for n0 in range(0, N_PROJ, PROJ_GROUP):
        n1 = min(n0 + PROJ_GROUP, N_PROJ)
        p_ref[:, n0:n1] = _dot(hb, win_ref[:, n0:n1]) + bin_ref[:, n0:n1]

    xc_ref[8:8 + tb, :] = p_ref[:, C_MQ:C_MQ + 512]
    conv = convb_ref[...] + convw_ref[3:4, :] * xc_ref[8:8 + tb, :]
    for j in range(3):
        conv = conv + convw_ref[j:j + 1, :] * xc_ref[5 + j:5 + j + tb, :]
    p_ref[:, C_MQ:C_MQ + 512] = _silu(conv)
    xc_ref[0:8, :] = xc_ref[tb:tb + 8, :]

    lg = lbl_ref[...]
    ex = jnp.exp(lg - jnp.max(lg, axis=0, keepdims=True))
    lb = jnp.zeros((1, 512), F32)
    for i in range(1, layer + 1):
        lb = lb + ex[i:i + 1, :]
    lb = lb / jnp.sum(ex, axis=0, keepdims=True)

    ones64 = jnp.ones((CHUNK, CHUNK), BF16)

    def gla_levels(q, k, x_all, ncols):
        qs, ks = [], []
        reps = ncols // 128
        for lvl in range(N_LEVELS):
            xl = x_all[128 + CHUNK * lvl:128 + CHUNK * (lvl + 1), :]
            qm = jnp.concatenate([qkrows_ref[0, lvl]] * reps, axis=1)
            km = jnp.concatenate([qkrows_ref[1, lvl]] * reps, axis=1)
            qs.append((q * (xl * qm)).astype(BF16))
            ks.append((k * (xl * km)).astype(BF16))
        qs.append(q.astype(BF16))
        ks.append(k.astype(BF16))
        return qs, ks

    def chunk_body(c, carry):
        r0 = pl.multiple_of(c * CHUNK, CHUNK)
        rows = pl.ds(r0, CHUNK)
        e_mat = e_ref[...]
        l_mat = e_mat[0:CHUNK, :]

        hq = p_ref[rows, C_HGQ:C_HGQ + 512]
        f = lb + (1.0 - lb) * _sigmoid(p_ref[rows, C_HGF:C_HGF + 512])
        q_hg = _silu(hq)
        k_hg = 1.0 - f
        v_hg = p_ref[rows, C_HGI:C_HGI + 512]
        small = p_ref[rows, C_SMALL:C_SMALL + 128]
        a = _dot(small.astype(BF16), wa2_ref[...]) + ba2_ref[...]
        g_gla = _log_sigmoid(a) / GLA_TAU
        q_g = p_ref[rows, C_GQ:C_GQ + 256] * (64 ** -0.5)
        k_g = p_ref[rows, C_GK:C_GK + 256]
        v_g = p_ref[rows, C_GV:C_GV + 512]
        g = jnp.concatenate([jnp.log(f), g_gla], axis=1)
        x_all = jnp.exp(_split_dot(e_mat, g))
        xb = x_all[0:CHUNK, :]
        xs = x_all[CHUNK:2 * CHUNK, :]
        xlast = xb[CHUNK - 1:CHUNK, :]

        qs, ks = gla_levels(q_hg, k_hg, x_all[:, 0:512], 512)
        q0 = (q_hg * xb[:, 0:512]).astype(BF16)
        khat = (k_hg * xs[:, 0:512]).astype(BF16)
        v_hg_b = v_hg.astype(BF16)
        zero = jnp.zeros((CHUNK, 128), BF16)
        o_pairs = []
        for p in range(2):
            lanes = slice(256 * p, 256 * (p + 1))
            att = jnp.zeros((CHUNK, 128), F32)
            for lvl in range(N_LEVELS + 1):
                kp = ks[lvl][:, lanes]
                kbd = jnp.concatenate([jnp.concatenate([kp[:, 0:128], zero], axis=1),
                                       jnp.concatenate([zero, kp[:, 128:256]], axis=1)], axis=0)
                part = _dot(qs[lvl][:, lanes], kbd, NT)
                att = att + (part if lvl == 0 else part * attmask_ref[lvl, :, 0:128])
            vp = v_hg_b[:, lanes]
            vbd = jnp.concatenate([jnp.concatenate([vp[:, 0:128], zero], axis=1),
                                   jnp.concatenate([zero, vp[:, 128:256]], axis=1)], axis=0)
            st = sthg_ref[p]
            o_pairs.append(_dot(att.astype(BF16), vbd) + _dot(q0[:, lanes], st.astype(BF16), NT))
            upd = _dot(vp, khat[:, lanes], TN)
            sthg_ref[p] = st * xlast[:, lanes] + upd * bdhg_ref[...]
        o_hg = jnp.concatenate(o_pairs, axis=1)
        y_ref[rows, 0:512] = _head_norm_gate(o_hg, p_ref[rows, C_HGZ:C_HGZ + 512], hgnw_ref[...]).astype(BF16)

        qs, ks = gla_levels(q_g, k_g, x_all[:, 512:768], 256)
        q0 = (q_g * xb[:, 512:768]).astype(BF16)
        khat = (k_g * xs[:, 512:768]).astype(BF16)
        v_g_b = v_g.astype(BF16)
        lane_head = [lanehead_ref[hh:hh + 1, :].astype(BF16) for hh in range(4)]

        def key_blockdiag(kk):
            return jnp.concatenate([kk * lane_head[hh] for hh in range(4)], axis=0)

        def val_blockdiag(vv):
            rows_ = []
            for hh in range(4):
                rows_.append(jnp.concatenate([vv[:, 128 * hh:128 * (hh + 1)] if j == hh else zero
                                              for j in range(4)], axis=1))
            return jnp.concatenate(rows_, axis=0)

        att = jnp.zeros((CHUNK, 256), F32)
        for lvl in range(N_LEVELS + 1):
            part = _dot(qs[lvl], key_blockdiag(ks[lvl]), NT)
            att = att + (part if lvl == 0 else part * attmask_ref[lvl])
        st = stg_ref[...]
        o_g = _dot(att.astype(BF16), val_blockdiag(v_g_b)) + _dot(q0, st.astype(BF16), NT)
        stg_ref[...] = st * xlast[:, 512:768] + _dot(v_g_b, khat, TN) * bdg_ref[...]
        y_ref[rows, 512:1024] = _head_norm_gate(o_g, p_ref[rows, C_GZ:C_GZ + 512], glanw_ref[...]).astype(BF16)

        lane = lax.broadcasted_iota(jnp.int32, (CHUNK, 128), 1)
        small2 = jnp.where((lane >= S_ML_F) & (lane < S_ML_F + 4), _log_sigmoid(small), small)
        ifx = _split_dot_rhs(small2, seg_ref[...])
        i_exp = ifx[:, 0:256]
        b_exp = _split_dot(l_mat, ifx[:, 256:512])
        r_exp = _split_dot(ones64, (b_exp - i_exp) * eye4_ref[...])
        dm = b_exp - r_exp + negm_ref[...]
        m_prev = nm_ref[1:2, :]
        rowmax = jnp.zeros((CHUNK, 256), F32)
        for hh in range(4):
            mh = jnp.max(dm[:, 64 * hh:64 * (hh + 1)], axis=-1, keepdims=True)
            rowmax = rowmax + mh * lanehead_ref[hh:hh + 1, :]
        m_t = jnp.maximum(b_exp + m_prev, rowmax)
        w = jnp.exp(dm - m_t)
        inter = jnp.exp(b_exp + m_prev - m_t)
        q_m = p_ref[rows, C_MQ:C_MQ + 256] * (64 ** -0.5)
        k_m = p_ref[rows, C_MK:C_MK + 256]
        v_m_b = p_ref[rows, C_MV:C_MV + 512].astype(BF16)
        q_m_b = q_m.astype(BF16)
        s_mat = _dot(q_m_b, key_blockdiag(k_m.astype(BF16)), NT) * w
        ctm = ct_ref[...]
        num_intra = _dot(s_mat.astype(BF16), val_blockdiag(v_m_b))
        num_inter = _dot(q_m_b, ctm.astype(BF16), NT)
        n_prev = nm_ref[0:1, :]
        qn = q_m * n_prev
        hs = []
        for hh in range(4):
            sl = slice(64 * hh, 64 * (hh + 1))
            inter_h = inter[:, 64 * hh:64 * hh + 1]
            den = inter_h * jnp.sum(qn[:, sl], axis=-1, keepdims=True) + jnp.sum(s_mat[:, sl], axis=-1, keepdims=True)
            lim = jnp.exp(-m_t[:, 64 * hh:64 * hh + 1])
            vs = slice(128 * hh, 128 * (hh + 1))
            hs.append((inter_h * num_inter[:, vs] + num_intra[:, vs]) / jnp.maximum(jnp.abs(den), lim))
        h_m = jnp.concatenate(hs, axis=1) * _sigmoid(p_ref[rows, C_MO:C_MO + 512])
        y_ref[rows, 1024:1536] = _head_norm_gate(h_m, p_ref[rows, C_MZ:C_MZ + 512], mlnw_ref[...]).astype(BF16)
        b_last = b_exp[CHUNK - 1:CHUNK, :]
        m_new = m_t[CHUNK - 1:CHUNK, :]
        khat_m = jnp.exp(b_last - b_exp + i_exp - m_new) * k_m
        scale = jnp.exp(b_last + m_prev - m_new)
        ct_ref[...] = ctm * scale + _dot(v_m_b, khat_m.astype(BF16), TN) * bdg_ref[...]
        nm_ref[0:1, :] = n_prev * scale + jnp.sum(khat_m, axis=0, keepdims=True)
        nm_ref[1:2, :] = m_new
        return carry

    lax.fori_loop(0, n_chunks, chunk_body, 0)

    out = x + _dot(y_ref[...], wout_ref[...])
    if apply_final_norm:
        out = out * lax.rsqrt(jnp.mean(out * out, axis=-1, keepdims=True) + NORM_EPS) * fnw_ref[...]
    out_ref[0] = out


def _split_dot_rhs(x, b_bf16):
    hi = x.astype(BF16)
    lo = (x - hi.astype(F32)).astype(BF16)
    return _dot(hi, b_bf16) + _dot(lo, b_bf16)


def _full(shape):
    nd = len(shape)
    return pl.BlockSpec(shape, lambda b, t: (0,) * nd, pipeline_mode=pl.Buffered(1))


def _layer_call(layer, apply_final_norm, x, consts, params):
    bsz, seq, d = x.shape
    tb = TIME_BLOCK
    assert seq % tb == 0 and tb % CHUNK == 0 and d == D_MODEL
    in_arrays = list(params) + [consts[k] for k in
                                ("e_mat", "qk_rows", "att_mask", "bd_hg", "bd_g", "lane_head", "seg", "eye4", "negm")]
    in_specs = [pl.BlockSpec((1, tb, d), lambda b, t: (b, t, 0))] + [_full(a.shape) for a in in_arrays]
    scratch = [
        pltpu.VMEM((tb, N_PROJ), F32),
        pltpu.VMEM((tb + 8, 512), F32),
        pltpu.VMEM((tb, 1536), BF16),
        pltpu.VMEM((2, 256, 256), F32),
        pltpu.VMEM((512, 256), F32),
        pltpu.VMEM((512, 256), F32),
        pltpu.VMEM((8, 256), F32),
    ]
    return pl.pallas_call(
        functools.partial(_layer_kernel, layer, apply_final_norm),
        grid=(bsz, seq // tb),
        in_specs=in_specs,
        out_specs=pl.BlockSpec((1, tb, d), lambda b, t: (b, t, 0)),
        out_shape=jax.ShapeDtypeStruct(x.shape, F32),
        scratch_shapes=scratch,
        compiler_params=pltpu.CompilerParams(dimension_semantics=("arbitrary", "arbitrary"),
                                             vmem_limit_bytes=VMEM_LIMIT_BYTES),
        name=f"trunk_layer{layer}",
    )(x, *in_arrays)


def _reorder_columns(w):
    pad = jnp.zeros(w.shape[:-1] + (N_PROJ - C_SMALL - 24,), w.dtype)
    return jnp.concatenate([w[..., 0:3072], w[..., 3088:4624], w[..., 4632:5656],
                            w[..., 3072:3088], w[..., 4624:4632], pad], axis=-1)


def kernel(x, norm_w, w_in, b_in, hg_lb_logits, hg_norm_w, gla_w_a2, gla_b_a2, gla_norm_w,
           ml_conv_w, ml_conv_b, ml_norm_w, w_out, final_norm_w):
    depth = w_in.shape[0]
    consts = _constants()
    w_in_r = _reorder_columns(w_in).astype(BF16)
    b_in_r = _reorder_columns(b_in)
    w_out_b = w_out.astype(BF16)
    wa2_pad = jnp.zeros((depth, 128, 256), F32).at[:, S_GLA_A:S_GLA_A + 16, :].set(gla_w_a2).astype(BF16)
    for l in range(depth):
        params = (norm_w[l][None, :], w_in_r[l], b_in_r[l][None, :], hg_lb_logits, hg_norm_w[l][None, :],
                  wa2_pad[l], gla_b_a2[l][None, :], gla_norm_w[l][None, :], ml_conv_w[l], ml_conv_b[l][None, :],
                  ml_norm_w[l][None, :], w_out_b[l], final_norm_w[None, :])
        x = _layer_call(l, l == depth - 1, x, consts, params)
    return x
```

```python
import functools
import math

import numpy as np
import jax
import jax.numpy as jnp
from jax import lax
from jax.experimental import pallas as pl
from jax.experimental.pallas import tpu as pltpu

D_MODEL = 1024
CHUNK = 64
NORM_EPS = 1e-6
N_LEVELS = 6
GLA_TAU = 16.0
LOG2E = math.log2(math.e)
TIME_BLOCK = 256
VMEM_LIMIT_BYTES = 56 * 1024 * 1024

C_HGQ, C_HGF, C_HGI, C_HGZ = 0, 512, 1024, 1536
C_GQ, C_GK, C_GV, C_GZ = 2048, 2304, 2560, 3072
C_MQ, C_MK, C_MV, C_MO, C_MZ = 3584, 3840, 4096, 4608, 5120
C_SMALL = 5632
N_PROJ = 5760
S_GLA_A, S_ML_I, S_ML_F = 0, 16, 20

F32 = jnp.float32
BF16 = jnp.bfloat16
NT = (((1,), (1,)), ((), ()))
TN = (((0,), (0,)), ((), ()))


def _dot(a, b, dims=None):
    if dims is None:
        return jnp.dot(a, b, preferred_element_type=F32)
    return lax.dot_general(a, b, dims, preferred_element_type=F32)


def _two_terms(x):
    hi = x.astype(BF16)
    return hi, (x - hi.astype(F32)).astype(BF16)


def _exact_lhs_dot(a2_bf16, x):
    return _dot(a2_bf16, jnp.concatenate(_two_terms(x), axis=0))


def _exact_rhs_dot(x, b2_bf16):
    return _dot(jnp.concatenate(_two_terms(x), axis=1), b2_bf16)


def _sigmoid(x):
    return 1.0 / (1.0 + jnp.exp(-x))


def _silu(x):
    return x * _sigmoid(x)


def _log_sigmoid(x):
    return jnp.minimum(x, 0.0) - jnp.log(1.0 + jnp.exp(-jnp.abs(x)))


def _constants():
    c = CHUNK
    t = np.arange(c)
    blocks = [(t[None, :] <= t[:, None]), (t[None, :] > t[:, None])]
    qrow, att = [], []
    for lvl in range(1, N_LEVELS + 1):
        s = c >> lvl
        pos = t % (2 * s)
        ref = (t // (2 * s)) * 2 * s + s - 1
        lo = np.minimum(t, ref)[:, None]
        hi = np.maximum(t, ref)[:, None]
        blocks.append((t[None, :] > lo) & (t[None, :] <= hi))
        qrow.append(pos >= s)
        same_pair = (t[:, None] // (2 * s)) == (t[None, :] // (2 * s))
        att.append(same_pair & (pos >= s)[:, None] & (pos < s)[None, :])
    att.append(np.eye(c, dtype=bool))
    e_mat = np.concatenate(blocks, axis=0).astype(np.float32)
    e2 = np.concatenate([e_mat, e_mat], axis=1)
    q_rows = np.repeat(np.stack(qrow)[:, :, None], 128, axis=2).astype(np.float32)
    att_mask = np.tile(np.stack(att).astype(np.float32), (1, 1, 4))
    lane_head = np.kron(np.eye(4), np.ones((1, 64))).astype(np.float32)
    seg = np.zeros((128, 512), np.float32)
    for h in range(4):
        seg[S_ML_I + h, 64 * h:64 * (h + 1)] = 1.0
        seg[S_ML_F + h, 256 + 64 * h:256 + 64 * (h + 1)] = 1.0
    seg2 = np.concatenate([seg, seg], axis=0)
    eye4 = np.tile(np.eye(c, dtype=np.float32), (1, 4))
    negm = np.tile(np.where(t[None, :] <= t[:, None], 0.0, -np.inf).astype(np.float32), (1, 4))
    return dict(e2=jnp.asarray(e2, BF16), q_rows=jnp.asarray(q_rows), att_mask=jnp.asarray(att_mask),
                lane_head=jnp.asarray(lane_head), seg2=jnp.asarray(seg2, BF16),
                eye4=jnp.asarray(eye4), negm=jnp.asarray(negm))


CONST_NAMES = ("e2", "q_rows", "att_mask", "lane_head", "seg2", "eye4", "negm")


def _head_norm(o):
    outs = []
    for h in range(4):
        oh = o[:, 128 * h:128 * (h + 1)]
        ms = jnp.mean(oh * oh, axis=-1, keepdims=True)
        outs.append(oh * lax.rsqrt(ms + NORM_EPS))
    return jnp.concatenate(outs, axis=1)


def _layer_kernel(layer, apply_final_norm, unroll_chunks,
                  x_ref, nw_ref, win_ref, bin_ref, lbl_ref, hgnw_ref, wa2_ref, ba2_ref, glanw_ref,
                  convw_ref, convb_ref, mlnw_ref, wout_ref, fnw_ref,
                  e2_ref, qrows_ref, attmask_ref, lanehead_ref, seg2_ref, eye4_ref, negm_ref,
                  out_ref,
                  p_ref, g_ref, mg_ref, xc_ref, y_ref, sthg_ref, stg_ref, ct_ref, nm_ref):
    tb = x_ref.shape[1]
    n_chunks = tb // CHUNK

    @pl.when(pl.program_id(1) == 0)
    def _():
        sthg_ref[...] = jnp.zeros_like(sthg_ref)
        stg_ref[...] = jnp.zeros_like(stg_ref)
        ct_ref[...] = jnp.zeros_like(ct_ref)
        nm_ref[...] = jnp.zeros_like(nm_ref)
        xc_ref[0:8, :] = jnp.zeros((8, 512), F32)

    lg = lbl_ref[...]
    ex = jnp.exp(lg - jnp.max(lg, axis=0, keepdims=True))
    lb = jnp.zeros((1, 512), F32)
    for i in range(1, layer + 1):
        lb = lb + ex[i:i + 1, :]
    lb = lb / jnp.sum(ex, axis=0, keepdims=True)

    x = x_ref[0]
    h = x * lax.rsqrt(jnp.mean(x * x, axis=-1, keepdims=True) + NORM_EPS) * nw_ref[...]
    hb = h.astype(BF16)

    def proj(c0, width):
        return _dot(hb, win_ref[:, c0:c0 + width]) + bin_ref[:, c0:c0 + width]

    small = proj(C_SMALL, 128)
    p_ref[:, C_SMALL:C_SMALL + 128] = small
    a = _dot(small.astype(BF16), wa2_ref[...]) + ba2_ref[...]
    g_ref[:, 512:768] = _log_sigmoid(a) * (LOG2E / GLA_TAU)
    lane = lax.broadcasted_iota(jnp.int32, (tb, 128), 1)
    small2 = jnp.where((lane >= S_ML_F) & (lane < S_ML_F + 4), _log_sigmoid(small), small)
    ifx = _exact_rhs_dot(small2, seg2_ref[...])
    mg_ref[:, 0:256] = ifx[:, 0:256]
    l2 = e2_ref[0:CHUNK, :]
    ones2 = jnp.ones((CHUNK, 2 * CHUNK), BF16)
    for c in range(n_chunks):
        rs = slice(c * CHUNK, (c + 1) * CHUNK)
        b_exp = _exact_lhs_dot(l2, ifx[rs, 256:512])
        mg_ref[rs, 256:512] = b_exp
        mg_ref[rs, 512:768] = _exact_lhs_dot(ones2, (b_exp - ifx[rs, 0:256]) * eye4_ref[...])

    f = lb + (1.0 - lb) * _sigmoid(proj(C_HGF, 512))
    p_ref[:, C_HGF:C_HGF + 512] = 1.0 - f
    g_ref[:, 0:512] = jnp.log(f) * LOG2E
    p_ref[:, C_HGQ:C_HGQ + 512] = _silu(proj(C_HGQ, 512))
    p_ref[:, C_HGI:C_HGI + 512] = proj(C_HGI, 512)
    gqk = proj(C_GQ, 512)
    p_ref[:, C_GQ:C_GQ + 256] = gqk[:, 0:256] * (64 ** -0.5)
    p_ref[:, C_GK:C_GK + 256] = gqk[:, 256:512]
    p_ref[:, C_GV:C_GV + 512] = proj(C_GV, 512)
    xc_ref[8:8 + tb, :] = proj(C_MQ, 512)
    conv = convb_ref[...] + convw_ref[3:4, :] * xc_ref[8:8 + tb, :]
    for j in range(3):
        conv = conv + convw_ref[j:j + 1, :] * xc_ref[5 + j:5 + j + tb, :]
    qk_m = _silu(conv)
    p_ref[:, C_MQ:C_MQ + 256] = qk_m[:, 0:256] * (64 ** -0.5)
    p_ref[:, C_MK:C_MK + 256] = qk_m[:, 256:512]
    xc_ref[0:8, :] = xc_ref[tb:tb + 8, :]
    p_ref[:, C_MV:C_MV + 512] = proj(C_MV, 512)
    p_ref[:, C_MO:C_MO + 512] = _sigmoid(proj(C_MO, 512))
    p_ref[:, C_HGZ:C_HGZ + 512] = _silu(proj(C_HGZ, 512)) * hgnw_ref[...]
    p_ref[:, C_GZ:C_GZ + 512] = _silu(proj(C_GZ, 512)) * glanw_ref[...]
    p_ref[:, C_MZ:C_MZ + 512] = _silu(proj(C_MZ, 512)) * mlnw_ref[...]

    zero = jnp.zeros((CHUNK, 128), BF16)
    lane128 = lax.broadcasted_iota(jnp.int32, (128, 128), 1)

    def gla_levels(q, k, x_all, ncols):
        reps = ncols // 128
        zs = []
        for lvl in range(N_LEVELS):
            xl = x_all[128 + CHUNK * lvl:128 + CHUNK * (lvl + 1), :]
            s = CHUNK >> (lvl + 1)
            if s >= 8:
                u = jnp.concatenate([(q if (i0 // s) % 2 else k)[i0:i0 + s, :] for i0 in range(0, CHUNK, s)], axis=0)
            else:
                qrow = jnp.concatenate([qrows_ref[lvl]] * reps, axis=1) > 0.5
                u = jnp.where(qrow, q, k)
            zs.append((u * xl).astype(BF16))
        return zs, q.astype(BF16), k.astype(BF16)

    def pair_blockdiag(m):
        return jnp.concatenate([jnp.concatenate([m[:, 0:128], zero], axis=1),
                                jnp.concatenate([zero, m[:, 128:256]], axis=1)], axis=0)

    def key_blockdiag(kk):
        return jnp.concatenate([kk * lanehead_ref[hh:hh + 1, :].astype(BF16) for hh in range(4)], axis=0)

    def val_blockdiag(vv):
        return jnp.concatenate([jnp.concatenate([vv[:, 128 * hh:128 * (hh + 1)] if j == hh else zero
                                                 for j in range(4)], axis=1) for hh in range(4)], axis=0)

    def packed_state_operand(st_ref):
        z128 = jnp.zeros((128, 128), BF16)
        blocks = []
        for p in range(2):
            sb = st_ref[p].astype(BF16)
            lo = jnp.where(lane128 < 64, sb, jnp.zeros_like(sb))
            hi = jnp.where(lane128 >= 64, sb, jnp.zeros_like(sb))
            for part in (lo, hi):
                blocks.append(jnp.concatenate([part, z128] if p == 0 else [z128, part], axis=1))
        return jnp.concatenate(blocks, axis=0)

    def packed_state_update(st_ref, decay, upd):
        for p in range(2):
            cols = slice(128 * p, 128 * (p + 1))
            own = jnp.where(lane128 < 64, upd[256 * p:256 * p + 128, cols], upd[256 * p + 128:256 * p + 256, cols])
            st_ref[p] = st_ref[p] * decay[:, cols] + own

    def chunk_body(c):
        r0 = c * CHUNK if isinstance(c, int) else pl.multiple_of(c * CHUNK, CHUNK)
        rows = pl.ds(r0, CHUNK)

        x_all = jnp.exp2(_exact_lhs_dot(e2_ref[...], g_ref[rows, :]))
        xb = x_all[0:CHUNK, :]
        xs = x_all[CHUNK:2 * CHUNK, :]
        xlast = xb[CHUNK - 1:CHUNK, :]

        q_hg = p_ref[rows, C_HGQ:C_HGQ + 512]
        k_hg = p_ref[rows, C_HGF:C_HGF + 512]
        zs, qd, kd = gla_levels(q_hg, k_hg, x_all[:, 0:512], 512)
        q0 = (q_hg * xb[:, 0:512]).astype(BF16)
        khat = (k_hg * xs[:, 0:512]).astype(BF16)
        v_hg_b = p_ref[rows, C_HGI:C_HGI + 512].astype(BF16)
        o_pairs = []
        for p in range(2):
            lanes = slice(256 * p, 256 * (p + 1))
            att = _dot(qd[:, lanes], pair_blockdiag(kd[:, lanes]), NT) * attmask_ref[N_LEVELS, :, 0:128]
            for lvl in range(N_LEVELS):
                zp = zs[lvl][:, lanes]
                att = att + _dot(zp, pair_blockdiag(zp), NT) * attmask_ref[lvl, :, 0:128]
            vp = v_hg_b[:, lanes]
            sa = sthg_ref[2 * p]
            sb = sthg_ref[2 * p + 1]
            z128 = jnp.zeros((128, 128), BF16)
            st_bd = jnp.concatenate([jnp.concatenate([sa.astype(BF16), z128], axis=1),
                                     jnp.concatenate([z128, sb.astype(BF16)], axis=1)], axis=0)
            o_pairs.append(_dot(att.astype(BF16), pair_blockdiag(vp)) + _dot(q0[:, lanes], st_bd, NT))
            upd = _dot(vp, khat[:, lanes], TN)
            sthg_ref[2 * p] = sa * xlast[:, 256 * p:256 * p + 128] + upd[0:128, 0:128]
            sthg_ref[2 * p + 1] = sb * xlast[:, 256 * p + 128:256 * p + 256] + upd[128:256, 128:256]
        o_hg = jnp.concatenate(o_pairs, axis=1)
        y_ref[rows, 0:512] = (_head_norm(o_hg) * p_ref[rows, C_HGZ:C_HGZ + 512]).astype(BF16)

        q_g = p_ref[rows, C_GQ:C_GQ + 256]
        k_g = p_ref[rows, C_GK:C_GK + 256]
        zs, qd, kd = gla_levels(q_g, k_g, x_all[:, 512:768], 256)
        q0 = (q_g * xb[:, 512:768]).astype(BF16)
        khat = (k_g * xs[:, 512:768]).astype(BF16)
        v_g_b = p_ref[rows, C_GV:C_GV + 512].astype(BF16)
        att = _dot(qd, key_blockdiag(kd), NT) * attmask_ref[N_LEVELS]
        for lvl in range(N_LEVELS):
            att = att + _dot(zs[lvl], key_blockdiag(zs[lvl]), NT) * attmask_ref[lvl]
        o_g = _dot(att.astype(BF16), val_blockdiag(v_g_b)) + _dot(q0, packed_state_operand(stg_ref), NT)
        packed_state_update(stg_ref, xlast[:, 512:768], _dot(v_g_b, khat, TN))
        y_ref[rows, 512:1024] = (_head_norm(o_g) * p_ref[rows, C_GZ:C_GZ + 512]).astype(BF16)

        i_exp = mg_ref[rows, 0:256]
        b_exp = mg_ref[rows, 256:512]
        dm = b_exp - mg_ref[rows, 512:768] + negm_ref[...]
        m_prev = nm_ref[1:2, :]
        rowmax = jnp.zeros((CHUNK, 256), F32)
        for hh in range(4):
            mh = jnp.max(dm[:, 64 * hh:64 * (hh + 1)], axis=-1, keepdims=True)
            rowmax = rowmax + mh * lanehead_ref[hh:hh + 1, :]
        m_t = jnp.maximum(b_exp + m_prev, rowmax)
        w = jnp.exp(dm - m_t)
        inter = jnp.exp(b_exp + m_prev - m_t)
        q_m = p_ref[rows, C_MQ:C_MQ + 256]
        k_m = p_ref[rows, C_MK:C_MK + 256]
        v_m_b = p_ref[rows, C_MV:C_MV + 512].astype(BF16)
        q_m_b = q_m.astype(BF16)
        s_mat = _dot(q_m_b, key_blockdiag(k_m.astype(BF16)), NT) * w
        num_intra = _dot(s_mat.astype(BF16), val_blockdiag(v_m_b))
        num_inter = _dot(q_m_b, packed_state_operand(ct_ref), NT)
        n_prev = nm_ref[0:1, :]
        qn = q_m * n_prev
        hs = []
        for hh in range(4):
            sl = slice(64 * hh, 64 * (hh + 1))
            inter_h = inter[:, 64 * hh:64 * hh + 1]
            den = inter_h * jnp.sum(qn[:, sl], axis=-1, keepdims=True) + jnp.sum(s_mat[:, sl], axis=-1, keepdims=True)
            lim = jnp.exp(-m_t[:, 64 * hh:64 * hh + 1])
            vs = slice(128 * hh, 128 * (hh + 1))
            hs.append((inter_h * num_inter[:, vs] + num_intra[:, vs]) / jnp.maximum(jnp.abs(den), lim))
        h_m = jnp.concatenate(hs, axis=1) * p_ref[rows, C_MO:C_MO + 512]
        y_ref[rows, 1024:1536] = (_head_norm(h_m) * p_ref[rows, C_MZ:C_MZ + 512]).astype(BF16)
        b_last = b_exp[CHUNK - 1:CHUNK, :]
        m_new = m_t[CHUNK - 1:CHUNK, :]
        khat_m = jnp.exp(b_last - b_exp + i_exp - m_new) * k_m
        scale = jnp.exp(b_last + m_prev - m_new)
        packed_state_update(ct_ref, scale, _dot(v_m_b, khat_m.astype(BF16), TN))
        nm_ref[0:1, :] = n_prev * scale + jnp.sum(khat_m, axis=0, keepdims=True)
        nm_ref[1:2, :] = m_new

    if unroll_chunks >= n_chunks:
        for c in range(n_chunks):
            chunk_body(c)
    else:
        def loop_body(i, carry):
            for j in range(unroll_chunks):
                chunk_body(i * unroll_chunks + j)
            return carry
        lax.fori_loop(0, n_chunks // unroll_chunks, loop_body, 0)

    out = x + _dot(y_ref[...], wout_ref[...])
    if apply_final_norm:
        out = out * lax.rsqrt(jnp.mean(out * out, axis=-1, keepdims=True) + NORM_EPS) * fnw_ref[...]
    out_ref[0] = out


def _full(shape):
    nd = len(shape)
    return pl.BlockSpec(shape, lambda b, t: (0,) * nd, pipeline_mode=pl.Buffered(1))


CHUNK_UNROLL = 4


def _layer_call(layer, apply_final_norm, x, consts, params):
    bsz, seq, d = x.shape
    tb = TIME_BLOCK
    assert seq % tb == 0 and tb % (CHUNK * CHUNK_UNROLL) == 0 and d == D_MODEL
    in_arrays = list(params) + [consts[k] for k in CONST_NAMES]
    in_specs = [pl.BlockSpec((1, tb, d), lambda b, t: (b, t, 0))] + [_full(a.shape) for a in in_arrays]
    scratch = [
        pltpu.VMEM((tb, N_PROJ), F32),
        pltpu.VMEM((tb, 768), F32),
        pltpu.VMEM((tb, 768), F32),
        pltpu.VMEM((tb + 8, 512), F32),
        pltpu.VMEM((tb, 1536), BF16),
        pltpu.VMEM((4, 128, 128), F32),
        pltpu.VMEM((2, 128, 128), F32),
        pltpu.VMEM((2, 128, 128), F32),
        pltpu.VMEM((8, 256), F32),
    ]
    return pl.pallas_call(
        functools.partial(_layer_kernel, layer, apply_final_norm, CHUNK_UNROLL),
        grid=(bsz, seq // tb),
        in_specs=in_specs,
        out_specs=pl.BlockSpec((1, tb, d), lambda b, t: (b, t, 0)),
        out_shape=jax.ShapeDtypeStruct(x.shape, F32),
        scratch_shapes=scratch,
        compiler_params=pltpu.CompilerParams(dimension_semantics=("arbitrary", "arbitrary"),
                                             vmem_limit_bytes=VMEM_LIMIT_BYTES),
        name=f"trunk_layer{layer}",
    )(x, *in_arrays)


def _reorder_columns(w):
    pad = jnp.zeros(w.shape[:-1] + (N_PROJ - C_SMALL - 24,), w.dtype)
    return jnp.concatenate([w[..., 0:3072], w[..., 3088:4624], w[..., 4632:5656],
                            w[..., 3072:3088], w[..., 4624:4632], pad], axis=-1)


def kernel(x, norm_w, w_in, b_in, hg_lb_logits, hg_norm_w, gla_w_a2, gla_b_a2, gla_norm_w,
           ml_conv_w, ml_conv_b, ml_norm_w, w_out, final_norm_w):
    depth = w_in.shape[0]
    consts = _constants()
    w_in_r = _reorder_columns(w_in).astype(BF16)
    b_in_r = _reorder_columns(b_in)
    w_out_b = w_out.astype(BF16)
    wa2_pad = jnp.zeros((depth, 128, 256), F32).at[:, S_GLA_A:S_GLA_A + 16, :].set(gla_w_a2).astype(BF16)
    for l in range(depth):
        params = (norm_w[l][None, :], w_in_r[l], b_in_r[l][None, :], hg_lb_logits, hg_norm_w[l][None, :],
                  wa2_pad[l], gla_b_a2[l][None, :], gla_norm_w[l][None, :], ml_conv_w[l], ml_conv_b[l][None, :],
                  ml_norm_w[l][None, :], w_out_b[l], final_norm_w[None, :])
        x = _layer_call(l, l == depth - 1, x, consts, params)
    return x
```

```python
import functools
import math

import numpy as np
import jax
import jax.numpy as jnp
from jax import lax
from jax.experimental import pallas as pl
from jax.experimental.pallas import tpu as pltpu

D_MODEL = 1024
CHUNK = 64
NORM_EPS = 1e-6
N_LEVELS = 6
GLA_TAU = 16.0
LOG2E = math.log2(math.e)
TIME_BLOCK = 256
VMEM_LIMIT_BYTES = 56 * 1024 * 1024

C_HGQ, C_HGF, C_HGI, C_HGZ = 0, 512, 1024, 1536
C_GQ, C_GK, C_GV, C_GZ = 2048, 2304, 2560, 3072
C_MQ, C_MK, C_MV, C_MO, C_MZ = 3584, 3840, 4096, 4608, 5120
C_SMALL = 5632
N_PROJ = 5760
S_GLA_A, S_ML_I, S_ML_F = 0, 16, 20

F32 = jnp.float32
BF16 = jnp.bfloat16
NT = (((1,), (1,)), ((), ()))
TN = (((0,), (0,)), ((), ()))


def _dot(a, b, dims=None):
    if dims is None:
        return jnp.dot(a, b, preferred_element_type=F32)
    return lax.dot_general(a, b, dims, preferred_element_type=F32)


def _two_terms(x):
    hi = x.astype(BF16)
    return hi, (x - hi.astype(F32)).astype(BF16)


def _exact_lhs_dot(a2_bf16, x):
    return _dot(a2_bf16, jnp.concatenate(_two_terms(x), axis=0))


def _exact_rhs_dot(x, b2_bf16):
    return _dot(jnp.concatenate(_two_terms(x), axis=1), b2_bf16)


def _sigmoid(x):
    return 1.0 / (1.0 + jnp.exp(-x))


def _silu(x):
    return x * _sigmoid(x)


def _log_sigmoid(x):
    return jnp.minimum(x, 0.0) - jnp.log(1.0 + jnp.exp(-jnp.abs(x)))


def _constants():
    c = CHUNK
    t = np.arange(c)
    prefix = t[None, :] <= t[:, None]
    fine, qrow, att = [], [], []
    for lvl in range(1, N_LEVELS + 1):
        s = c >> lvl
        pos = t % (2 * s)
        ref = (t // (2 * s)) * 2 * s + s - 1
        lo = np.minimum(t, ref)[:, None]
        hi = np.maximum(t, ref)[:, None]
        if s < 8:
            fine.append((t[None, :] > lo) & (t[None, :] <= hi))
        qrow.append(pos >= s)
        same_pair = (t[:, None] // (2 * s)) == (t[None, :] // (2 * s))
        att.append(same_pair & (pos >= s)[:, None] & (pos < s)[None, :])
    att.append(np.eye(c, dtype=bool))
    e_mat = np.concatenate([prefix] + fine, axis=0).astype(np.float32)
    e2 = np.concatenate([e_mat, e_mat], axis=1)
    q_rows = np.repeat(np.stack(qrow)[:, :, None], 128, axis=2).astype(np.float32)
    att_mask = np.tile(np.stack(att).astype(np.float32), (1, 1, 2))
    lane_head = np.kron(np.eye(4), np.ones((1, 64))).astype(np.float32)
    seg = np.zeros((128, 512), np.float32)
    for h in range(4):
        seg[S_ML_I + h, 64 * h:64 * (h + 1)] = 1.0
        seg[S_ML_F + h, 256 + 64 * h:256 + 64 * (h + 1)] = 1.0
    seg2 = np.concatenate([seg, seg], axis=0)
    eye4 = np.tile(np.eye(c, dtype=np.float32), (1, 4))
    negm = np.tile(np.where(t[None, :] <= t[:, None], 0.0, -np.inf).astype(np.float32), (1, 4))
    return dict(e2=jnp.asarray(e2, BF16), q_rows=jnp.asarray(q_rows), att_mask=jnp.asarray(att_mask),
                lane_head=jnp.asarray(lane_head), seg2=jnp.asarray(seg2, BF16),
                eye4=jnp.asarray(eye4), negm=jnp.asarray(negm))


CONST_NAMES = ("e2", "q_rows", "att_mask", "lane_head", "seg2", "eye4", "negm")


def _head_norm(o):
    outs = []
    for h in range(4):
        oh = o[:, 128 * h:128 * (h + 1)]
        ms = jnp.mean(oh * oh, axis=-1, keepdims=True)
        outs.append(oh * lax.rsqrt(ms + NORM_EPS))
    return jnp.concatenate(outs, axis=1)


def _layer_kernel(layer, apply_final_norm,
                  x_ref, nw_ref, win_ref, bin_ref, hgnw_ref, wa2_ref, ba2_ref, glanw_ref,
                  convw_ref, convb_ref, mlnw_ref, wout_ref, lbl_ref, fnw_ref,
                  e2_ref, qrows_ref, attmask_ref, lanehead_ref, seg2_ref, eye4_ref, negm_ref,
                  out_ref,
                  p_ref, g_ref, mg_ref, xc_ref, y_ref, sthg_ref, stg_ref, ct_ref, nm_ref):
    tb = x_ref.shape[1]
    n_chunks = tb // CHUNK

    @pl.when(pl.program_id(1) == 0)
    def _():
        sthg_ref[...] = jnp.zeros_like(sthg_ref)
        stg_ref[...] = jnp.zeros_like(stg_ref)
        ct_ref[...] = jnp.zeros_like(ct_ref)
        nm_ref[...] = jnp.zeros_like(nm_ref)
        xc_ref[0:8, :] = jnp.zeros((8, 512), F32)

    lg = lbl_ref[...]
    ex = jnp.exp(lg - jnp.max(lg, axis=0, keepdims=True))
    lb = jnp.zeros((1, 512), F32)
    for i in range(1, layer + 1):
        lb = lb + ex[i:i + 1, :]
    lb = lb / jnp.sum(ex, axis=0, keepdims=True)

    x = x_ref[0]
    h = x * lax.rsqrt(jnp.mean(x * x, axis=-1, keepdims=True) + NORM_EPS) * nw_ref[...]
    hb = h.astype(BF16)

    def proj(c0, width):
        return _dot(hb, win_ref[:, c0:c0 + width]) + bin_ref[:, c0:c0 + width]

    small = proj(C_SMALL, 128)
    a = _dot(small.astype(BF16), wa2_ref[...]) + ba2_ref[...]
    g_ref[:, 512:768] = _log_sigmoid(a) * (LOG2E / GLA_TAU)
    lane = lax.broadcasted_iota(jnp.int32, (tb, 128), 1)
    small2 = jnp.where((lane >= S_ML_F) & (lane < S_ML_F + 4), _log_sigmoid(small), small)
    ifx = _exact_rhs_dot(small2, seg2_ref[...])
    mg_ref[:, 0:256] = ifx[:, 0:256]
    l2 = e2_ref[0:CHUNK, :]
    ones2 = jnp.ones((CHUNK, 2 * CHUNK), BF16)
    for c in range(n_chunks):
        rs = slice(c * CHUNK, (c + 1) * CHUNK)
        b_exp = _exact_lhs_dot(l2, ifx[rs, 256:512])
        mg_ref[rs, 256:512] = b_exp
        mg_ref[rs, 512:768] = _exact_lhs_dot(ones2, (b_exp - ifx[rs, 0:256]) * eye4_ref[...])

    f = lb + (1.0 - lb) * _sigmoid(proj(C_HGF, 512))
    p_ref[:, C_HGF:C_HGF + 512] = 1.0 - f
    g_ref[:, 0:512] = jnp.log(f) * LOG2E
    p_ref[:, C_HGQ:C_HGQ + 512] = _silu(proj(C_HGQ, 512))
    gqk = proj(C_GQ, 512)
    p_ref[:, C_GQ:C_GQ + 256] = gqk[:, 0:256] * (64 ** -0.5)
    p_ref[:, C_GK:C_GK + 256] = gqk[:, 256:512]

    def project_rest():
        p_ref[:, C_HGI:C_HGI + 512] = proj(C_HGI, 512)
        p_ref[:, C_GV:C_GV + 512] = proj(C_GV, 512)
        xc_ref[8:8 + tb, :] = proj(C_MQ, 512)
        conv = convb_ref[...] + convw_ref[3:4, :] * xc_ref[8:8 + tb, :]
        for j in range(3):
            conv = conv + convw_ref[j:j + 1, :] * xc_ref[5 + j:5 + j + tb, :]
        qk_m = _silu(conv)
        p_ref[:, C_MQ:C_MQ + 256] = qk_m[:, 0:256] * (64 ** -0.5)
        p_ref[:, C_MK:C_MK + 256] = qk_m[:, 256:512]
        xc_ref[0:8, :] = xc_ref[tb:tb + 8, :]
        p_ref[:, C_MV:C_MV + 512] = proj(C_MV, 512)
        p_ref[:, C_MO:C_MO + 512] = _sigmoid(proj(C_MO, 512))
        p_ref[:, C_HGZ:C_HGZ + 512] = _silu(proj(C_HGZ, 512)) * hgnw_ref[...]
        p_ref[:, C_GZ:C_GZ + 512] = _silu(proj(C_GZ, 512)) * glanw_ref[...]
        p_ref[:, C_MZ:C_MZ + 512] = _silu(proj(C_MZ, 512)) * mlnw_ref[...]

    zero = jnp.zeros((CHUNK, 128), BF16)
    lane128 = lax.broadcasted_iota(jnp.int32, (128, 128), 1)
    lane_lo = lax.broadcasted_iota(jnp.int32, (CHUNK, 128), 1) < 64

    def decay_factors(a_small):
        b = a_small[0:CHUNK, :]
        xb = jnp.exp2(b)
        xs = jnp.exp2(b[CHUNK - 1:CHUNK, :] - b)
        levels = []
        for lvl in range(3):
            s = CHUNK >> (lvl + 1)
            parts = []
            for i0 in range(0, CHUNK, 2 * s):
                ref = b[i0 + s - 1:i0 + s, :]
                parts.append(ref - b[i0:i0 + s, :])
                parts.append(b[i0 + s:i0 + 2 * s, :] - ref)
            levels.append(jnp.exp2(jnp.concatenate(parts, axis=0)))
        for lvl in range(3):
            levels.append(jnp.exp2(a_small[CHUNK * (lvl + 1):CHUNK * (lvl + 2), :]))
        return xb, xs, levels

    def gla_levels(q, k, levels, ncols):
        reps = ncols // 128
        zs = []
        for lvl in range(N_LEVELS):
            s = CHUNK >> (lvl + 1)
            if s >= 8:
                u = jnp.concatenate([(q if (i0 // s) % 2 else k)[i0:i0 + s, :] for i0 in range(0, CHUNK, s)], axis=0)
            else:
                qrow = jnp.concatenate([qrows_ref[lvl]] * reps, axis=1) > 0.5
                u = jnp.where(qrow, q, k)
            zs.append((u * levels[lvl]).astype(BF16))
        return zs, q.astype(BF16), k.astype(BF16)

    def pair_blockdiag(m):
        return jnp.concatenate([jnp.concatenate([m[:, 0:128], zero], axis=1),
                                jnp.concatenate([zero, m[:, 128:256]], axis=1)], axis=0)

    def halves_blockdiag(m):
        zeros = jnp.zeros_like(m)
        return jnp.concatenate([jnp.where(lane_lo, m, zeros), jnp.where(lane_lo, zeros, m)], axis=0)

    def packed_state_operand(st):
        sb = st.astype(BF16)
        zeros = jnp.zeros_like(sb)
        return jnp.concatenate([jnp.where(lane128 < 64, sb, zeros), jnp.where(lane128 < 64, zeros, sb)], axis=0)

    def packed_own(upd):
        return jnp.where(lane128 < 64, upd[0:128, :], upd[128:256, :])

    def stage_factors(c):
        rows = pl.ds(c * CHUNK, CHUNK)
        xb, xs, levels = decay_factors(_exact_lhs_dot(e2_ref[...], g_ref[rows, :]))
        d = dict(rows=rows, xlast=xb[CHUNK - 1:CHUNK, :])
        q_hg = p_ref[rows, C_HGQ:C_HGQ + 512]
        k_hg = p_ref[rows, C_HGF:C_HGF + 512]
        d["hg"] = gla_levels(q_hg, k_hg, [lv[:, 0:512] for lv in levels], 512)
        d["hg_q0"] = (q_hg * xb[:, 0:512]).astype(BF16)
        d["hg_khat"] = (k_hg * xs[:, 0:512]).astype(BF16)
        q_g = p_ref[rows, C_GQ:C_GQ + 256]
        k_g = p_ref[rows, C_GK:C_GK + 256]
        d["g"] = gla_levels(q_g, k_g, [lv[:, 512:768] for lv in levels], 256)
        d["g_q0"] = (q_g * xb[:, 512:768]).astype(BF16)
        d["g_khat"] = (k_g * xs[:, 512:768]).astype(BF16)
        return d

    def stage_mlstm_factors(d):
        rows = d["rows"]
        i_exp = mg_ref[rows, 0:256]
        b_exp = mg_ref[rows, 256:512]
        dm = b_exp - mg_ref[rows, 512:768] + negm_ref[...]
        m_loc = jnp.zeros((CHUNK, 256), F32)
        for hh in range(4):
            mh = jnp.max(dm[:, 64 * hh:64 * (hh + 1)], axis=-1, keepdims=True)
            m_loc = m_loc + mh * lanehead_ref[hh:hh + 1, :]
        b_last = b_exp[CHUNK - 1:CHUNK, :]
        a_upd = b_last - b_exp + i_exp
        mk = jnp.max(a_upd, axis=0, keepdims=True)
        d.update(b_exp=b_exp, b_last=b_last, m_loc=m_loc, mk=mk, w_loc=jnp.exp(dm - m_loc),
                 khat_m=jnp.exp(a_upd - mk) * p_ref[rows, C_MK:C_MK + 256])

    def stage_scores(d):
        rows = d["rows"]
        zs, qd, kd = d["hg"]
        v_hg_b = p_ref[rows, C_HGI:C_HGI + 512].astype(BF16)
        att_hg, u_hg = [], []
        for p in range(2):
            lanes = slice(256 * p, 256 * (p + 1))
            att = _dot(qd[:, lanes], pair_blockdiag(kd[:, lanes]), NT) * attmask_ref[N_LEVELS]
            for lvl in range(N_LEVELS):
                zp = zs[lvl][:, lanes]
                att = att + _dot(zp, pair_blockdiag(zp), NT) * attmask_ref[lvl]
            att_hg.append(att.astype(BF16))
            upd = _dot(v_hg_b[:, lanes], d["hg_khat"][:, lanes], TN)
            u_hg += [upd[0:128, 0:128], upd[128:256, 128:256]]
        zs, qd, kd = d["g"]
        v_g_b = p_ref[rows, C_GV:C_GV + 512].astype(BF16)
        att_g, u_g = [], []
        for p in range(2):
            lanes = slice(128 * p, 128 * (p + 1))
            att = _dot(qd[:, lanes], halves_blockdiag(kd[:, lanes]), NT) * attmask_ref[N_LEVELS]
            for lvl in range(N_LEVELS):
                zp = zs[lvl][:, lanes]
                att = att + _dot(zp, halves_blockdiag(zp), NT) * attmask_ref[lvl]
            att_g.append(att.astype(BF16))
            u_g.append(packed_own(_dot(v_g_b[:, 256 * p:256 * (p + 1)], d["g_khat"][:, lanes], TN)))
        q_m_b = p_ref[rows, C_MQ:C_MQ + 256].astype(BF16)
        k_m_b = p_ref[rows, C_MK:C_MK + 256].astype(BF16)
        v_m_b = p_ref[rows, C_MV:C_MV + 512].astype(BF16)
        khat_m_b = d["khat_m"].astype(BF16)
        s_m, u_m, rs_m = [], [], []
        for p in range(2):
            lanes = slice(128 * p, 128 * (p + 1))
            s_loc = _dot(q_m_b[:, lanes], halves_blockdiag(k_m_b[:, lanes]), NT) * d["w_loc"][:, lanes]
            s_m.append(s_loc.astype(BF16))
            u_m.append(packed_own(_dot(v_m_b[:, 256 * p:256 * (p + 1)], khat_m_b[:, lanes], TN)))
            rs_m += [jnp.sum(s_loc[:, 0:64], axis=-1, keepdims=True), jnp.sum(s_loc[:, 64:128], axis=-1, keepdims=True)]
        d.update(att_hg=att_hg, u_hg=u_hg, att_g=att_g, u_g=u_g, s_m=s_m, u_m=u_m, rs_m=rs_m,
                 q_m_b=q_m_b, v_hg_b=v_hg_b, v_g_b=v_g_b, v_m_b=v_m_b)

    def stage_values(d):
        d["o_hg"] = [_dot(d["att_hg"][p], pair_blockdiag(d["v_hg_b"][:, 256 * p:256 * (p + 1)])) for p in range(2)]
        d["o_g"] = [_dot(d["att_g"][p], pair_blockdiag(d["v_g_b"][:, 256 * p:256 * (p + 1)])) for p in range(2)]
        d["o_m"] = [_dot(d["s_m"][p], pair_blockdiag(d["v_m_b"][:, 256 * p:256 * (p + 1)])) for p in range(2)]

    def stage_state(d):
        rows = d["rows"]
        xlast = d["xlast"]
        o_heads = []
        for hh in range(4):
            hl = slice(128 * hh, 128 * (hh + 1))
            st = sthg_ref[hh]
            o_heads.append(d["o_hg"][hh // 2][:, 128 * (hh % 2):128 * (hh % 2 + 1)]
                           + _dot(d["hg_q0"][:, hl], st.astype(BF16), NT))
            sthg_ref[hh] = st * xlast[:, hl] + d["u_hg"][hh]
        y_ref[rows, 0:512] = (_head_norm(jnp.concatenate(o_heads, axis=1)) * p_ref[rows, C_HGZ:C_HGZ + 512]).astype(BF16)
        o_pairs = []
        for p in range(2):
            lanes = slice(128 * p, 128 * (p + 1))
            st = stg_ref[p]
            o_pairs.append(d["o_g"][p] + _dot(d["g_q0"][:, lanes], packed_state_operand(st), NT))
            stg_ref[p] = st * xlast[:, 512 + 128 * p:512 + 128 * (p + 1)] + d["u_g"][p]
        y_ref[rows, 512:1024] = (_head_norm(jnp.concatenate(o_pairs, axis=1)) * p_ref[rows, C_GZ:C_GZ + 512]).astype(BF16)
        b_exp, m_loc = d["b_exp"], d["m_loc"]
        m_prev = nm_ref[1:2, :]
        n_prev = nm_ref[0:1, :]
        m_t = jnp.maximum(b_exp + m_prev, m_loc)
        corr = jnp.exp(m_loc - m_t)
        inter = jnp.exp(b_exp + m_prev - m_t)
        m_new = m_t[CHUNK - 1:CHUNK, :]
        scale = jnp.exp(d["b_last"] + m_prev - m_new)
        upd_scale = jnp.exp(d["mk"] - m_new)
        qn = p_ref[rows, C_MQ:C_MQ + 256] * n_prev
        hs = []
        for p in range(2):
            lanes = slice(128 * p, 128 * (p + 1))
            ctm = ct_ref[p]
            num_inter = _dot(d["q_m_b"][:, lanes], packed_state_operand(ctm), NT)
            ct_ref[p] = ctm * scale[:, lanes] + d["u_m"][p] * upd_scale[:, lanes]
            for j in range(2):
                hh = 2 * p + j
                col = slice(64 * hh, 64 * hh + 1)
                den = (inter[:, col] * jnp.sum(qn[:, 64 * hh:64 * (hh + 1)], axis=-1, keepdims=True)
                       + corr[:, col] * d["rs_m"][hh])
                vs = slice(128 * j, 128 * (j + 1))
                num = inter[:, col] * num_inter[:, vs] + corr[:, col] * d["o_m"][p][:, vs]
                hs.append(num / jnp.maximum(jnp.abs(den), jnp.exp(-m_t[:, col])))
        h_m = jnp.concatenate(hs, axis=1) * p_ref[rows, C_MO:C_MO + 512]
        y_ref[rows, 1024:1536] = (_head_norm(h_m) * p_ref[rows, C_MZ:C_MZ + 512]).astype(BF16)
        nm_ref[0:1, :] = n_prev * scale + jnp.sum(d["khat_m"], axis=0, keepdims=True) * upd_scale
        nm_ref[1:2, :] = m_new

    chunks = [stage_factors(c) for c in range(n_chunks)]
    project_rest()
    for d in chunks:
        stage_mlstm_factors(d)
    for d in chunks:
        stage_scores(d)
    for d in chunks:
        stage_values(d)
    for d in chunks:
        stage_state(d)

    out = x + _dot(y_ref[...], wout_ref[...])
    if apply_final_norm:
        out = out * lax.rsqrt(jnp.mean(out * out, axis=-1, keepdims=True) + NORM_EPS) * fnw_ref[...]
    out_ref[0] = out


def _whole(arr):
    nd = arr.ndim
    return pl.BlockSpec(arr.shape, lambda b, t: (0,) * nd, pipeline_mode=pl.Buffered(1))


def _slab(arr, layer):
    nd = arr.ndim - 1
    return pl.BlockSpec((None,) + arr.shape[1:], lambda b, t: (layer,) + (0,) * nd, pipeline_mode=pl.Buffered(1))


def _layer_call(layer, apply_final_norm, x, stacked, shared, consts):
    bsz, seq, d = x.shape
    tb = TIME_BLOCK
    assert seq % tb == 0 and tb % CHUNK == 0 and d == D_MODEL
    const_arrays = [consts[k] for k in CONST_NAMES]
    in_specs = ([pl.BlockSpec((1, tb, d), lambda b, t: (b, t, 0))]
                + [_slab(a, layer) for a in stacked] + [_whole(a) for a in shared] + [_whole(a) for a in const_arrays])
    scratch = [
        pltpu.VMEM((tb, N_PROJ), F32),
        pltpu.VMEM((tb, 768), F32),
        pltpu.VMEM((tb, 768), F32),
        pltpu.VMEM((tb + 8, 512), F32),
        pltpu.VMEM((tb, 1536), BF16),
        pltpu.VMEM((4, 128, 128), F32),
        pltpu.VMEM((2, 128, 128), F32),
        pltpu.VMEM((2, 128, 128), F32),
        pltpu.VMEM((8, 256), F32),
    ]
    return pl.pallas_call(
        functools.partial(_layer_kernel, layer, apply_final_norm),
        grid=(bsz, seq // tb),
        in_specs=in_specs,
        out_specs=pl.BlockSpec((1, tb, d), lambda b, t: (b, t, 0)),
        out_shape=jax.ShapeDtypeStruct(x.shape, F32),
        scratch_shapes=scratch,
        compiler_params=pltpu.CompilerParams(dimension_semantics=("arbitrary", "arbitrary"),
                                             vmem_limit_bytes=VMEM_LIMIT_BYTES),
        name=f"trunk_layer{layer}",
    )(x, *stacked, *shared, *const_arrays)


def _reorder_columns(w):
    pad = jnp.zeros(w.shape[:-1] + (N_PROJ - C_SMALL - 24,), w.dtype)
    return jnp.concatenate([w[..., 0:3072], w[..., 3088:4624], w[..., 4632:5656],
                            w[..., 3072:3088], w[..., 4624:4632], pad], axis=-1)


def kernel(x, norm_w, w_in, b_in, hg_lb_logits, hg_norm_w, gla_w_a2, gla_b_a2, gla_norm_w,
           ml_conv_w, ml_conv_b, ml_norm_w, w_out, final_norm_w):
    depth = w_in.shape[0]
    consts = _constants()
    wa2_pad = jnp.zeros((depth, 128, 256), F32).at[:, S_GLA_A:S_GLA_A + 16, :].set(gla_w_a2).astype(BF16)
    row = lambda v: v[:, None, :]
    stacked = (row(norm_w), _reorder_columns(w_in.astype(BF16)), row(_reorder_columns(b_in)), row(hg_norm_w),
               wa2_pad, row(gla_b_a2), row(gla_norm_w), ml_conv_w, row(ml_conv_b), row(ml_norm_w),
               w_out.astype(BF16))
    shared = (hg_lb_logits, final_norm_w[None, :])
    for l in range(depth):
        x = _layer_call(l, l == depth - 1, x, stacked, shared, consts)
    return x
```

```python
import functools
import math

import numpy as np
import jax
import jax.numpy as jnp
from jax import lax
from jax.experimental import pallas as pl
from jax.experimental.pallas import tpu as pltpu

D_MODEL = 1024
CHUNK = 64
NORM_EPS = 1e-6
N_LEVELS = 6
GLA_TAU = 16.0
LOG2E = math.log2(math.e)
TIME_BLOCK = 256
VMEM_LIMIT_BYTES = 56 * 1024 * 1024

C_HGQ, C_HGF, C_HGI, C_HGZ = 0, 512, 1024, 1536
C_GQ, C_GK, C_GV, C_GZ = 2048, 2304, 2560, 3072
C_MQ, C_MK, C_MV, C_MO, C_MZ = 3584, 3840, 4096, 4608, 5120
C_SMALL = 5632
N_PROJ = 5760
S_GLA_A, S_ML_I, S_ML_F = 0, 16, 20

F32 = jnp.float32
BF16 = jnp.bfloat16
NT = (((1,), (1,)), ((), ()))
TN = (((0,), (0,)), ((), ()))


def _dot(a, b, dims=None):
    if dims is None:
        return jnp.dot(a, b, preferred_element_type=F32)
    return lax.dot_general(a, b, dims, preferred_element_type=F32)


def _two_terms(x):
    hi = x.astype(BF16)
    return hi, (x - hi.astype(F32)).astype(BF16)


def _exact_lhs_dot(a2_bf16, x):
    return _dot(a2_bf16, jnp.concatenate(_two_terms(x), axis=0))


def _exact_rhs_dot(x, b2_bf16):
    return _dot(jnp.concatenate(_two_terms(x), axis=1), b2_bf16)


def _sigmoid(x):
    return 1.0 / (1.0 + jnp.exp(-x))


def _silu(x):
    return x * _sigmoid(x)


def _log_sigmoid(x):
    return jnp.minimum(x, 0.0) - jnp.log(1.0 + jnp.exp(-jnp.abs(x)))


def _constants():
    c = CHUNK
    t = np.arange(c)
    prefix = t[None, :] <= t[:, None]
    fine, qrow, att = [], [], []
    for lvl in range(1, N_LEVELS + 1):
        s = c >> lvl
        pos = t % (2 * s)
        ref = (t // (2 * s)) * 2 * s + s - 1
        lo = np.minimum(t, ref)[:, None]
        hi = np.maximum(t, ref)[:, None]
        if s < 8:
            fine.append((t[None, :] > lo) & (t[None, :] <= hi))
        qrow.append(pos >= s)
        same_pair = (t[:, None] // (2 * s)) == (t[None, :] // (2 * s))
        att.append(same_pair & (pos >= s)[:, None] & (pos < s)[None, :])
    att.append(np.eye(c, dtype=bool))
    e_mat = np.concatenate([prefix] + fine, axis=0).astype(np.float32)
    e2 = np.concatenate([e_mat, e_mat], axis=1)
    q_rows = np.repeat(np.stack(qrow)[:, :, None], 128, axis=2).astype(np.float32)
    att_mask = np.tile(np.stack(att).astype(np.float32), (1, 1, 2))
    lane_head = np.kron(np.eye(4), np.ones((1, 64))).astype(np.float32)
    seg = np.zeros((128, 512), np.float32)
    for h in range(4):
        seg[S_ML_I + h, 64 * h:64 * (h + 1)] = 1.0
        seg[S_ML_F + h, 256 + 64 * h:256 + 64 * (h + 1)] = 1.0
    seg2 = np.concatenate([seg, seg], axis=0)
    eye4 = np.tile(np.eye(c, dtype=np.float32), (1, 4))
    negm = np.tile(np.where(t[None, :] <= t[:, None], 0.0, -np.inf).astype(np.float32), (1, 4))
    return dict(e2=jnp.asarray(e2, BF16), q_rows=jnp.asarray(q_rows), att_mask=jnp.asarray(att_mask),
                lane_head=jnp.asarray(lane_head), seg2=jnp.asarray(seg2, BF16),
                eye4=jnp.asarray(eye4), negm=jnp.asarray(negm))


CONST_NAMES = ("e2", "q_rows", "att_mask", "lane_head", "seg2", "eye4", "negm")


def _head_norm(o):
    outs = []
    for h in range(4):
        oh = o[:, 128 * h:128 * (h + 1)]
        ms = jnp.mean(oh * oh, axis=-1, keepdims=True)
        outs.append(oh * lax.rsqrt(ms + NORM_EPS))
    return jnp.concatenate(outs, axis=1)


def _layer_kernel(layer, apply_final_norm,
                  x_ref, nw_ref, win_ref, bin_ref, hgnw_ref, wa2_ref, ba2_ref, glanw_ref,
                  convw_ref, convb_ref, mlnw_ref, wout_ref, lbl_ref, fnw_ref,
                  e2_ref, qrows_ref, attmask_ref, lanehead_ref, seg2_ref, eye4_ref, negm_ref,
                  out_ref,
                  p_ref, g_ref, mg_ref, xc_ref, y_ref, sthg_ref, stg_ref, ct_ref, nm_ref):
    tb = x_ref.shape[1]
    n_chunks = tb // CHUNK

    @pl.when(pl.program_id(1) == 0)
    def _():
        sthg_ref[...] = jnp.zeros_like(sthg_ref)
        stg_ref[...] = jnp.zeros_like(stg_ref)
        ct_ref[...] = jnp.zeros_like(ct_ref)
        nm_ref[...] = jnp.zeros_like(nm_ref)
        xc_ref[0:8, :] = jnp.zeros((8, 512), F32)

    lg = lbl_ref[...]
    ex = jnp.exp(lg - jnp.max(lg, axis=0, keepdims=True))
    lb = jnp.zeros((1, 512), F32)
    for i in range(1, layer + 1):
        lb = lb + ex[i:i + 1, :]
    lb = lb / jnp.sum(ex, axis=0, keepdims=True)

    x = x_ref[0]
    h = x * lax.rsqrt(jnp.mean(x * x, axis=-1, keepdims=True) + NORM_EPS) * nw_ref[...]
    hb = h.astype(BF16)

    def proj(c0, width):
        return _dot(hb, win_ref[:, c0:c0 + width]) + bin_ref[:, c0:c0 + width]

    small = proj(C_SMALL, 128)
    a = _dot(small.astype(BF16), wa2_ref[...]) + ba2_ref[...]
    g_ref[:, 512:768] = _log_sigmoid(a) * (LOG2E / GLA_TAU)
    lane = lax.broadcasted_iota(jnp.int32, (tb, 128), 1)
    small2 = jnp.where((lane >= S_ML_F) & (lane < S_ML_F + 4), _log_sigmoid(small), small)
    ifx = _exact_rhs_dot(small2, seg2_ref[...])
    mg_ref[:, 0:256] = ifx[:, 0:256]
    l2 = e2_ref[0:CHUNK, :]
    ones2 = jnp.ones((CHUNK, 2 * CHUNK), BF16)
    for c in range(n_chunks):
        rs = slice(c * CHUNK, (c + 1) * CHUNK)
        b_exp = _exact_lhs_dot(l2, ifx[rs, 256:512])
        mg_ref[rs, 256:512] = b_exp
        mg_ref[rs, 512:768] = _exact_lhs_dot(ones2, (b_exp - ifx[rs, 0:256]) * eye4_ref[...])

    f = lb + (1.0 - lb) * _sigmoid(proj(C_HGF, 512))
    p_ref[:, C_HGF:C_HGF + 512] = 1.0 - f
    g_ref[:, 0:512] = jnp.log(f) * LOG2E
    p_ref[:, C_HGQ:C_HGQ + 512] = _silu(proj(C_HGQ, 512))
    gqk = proj(C_GQ, 512)
    p_ref[:, C_GQ:C_GQ + 256] = gqk[:, 0:256] * (64 ** -0.5)
    p_ref[:, C_GK:C_GK + 256] = gqk[:, 256:512]

    def project_rest():
        p_ref[:, C_HGI:C_HGI + 512] = proj(C_HGI, 512)
        p_ref[:, C_GV:C_GV + 512] = proj(C_GV, 512)
        xc_ref[8:8 + tb, :] = proj(C_MQ, 512)
        conv = convb_ref[...] + convw_ref[3:4, :] * xc_ref[8:8 + tb, :]
        for j in range(3):
            conv = conv + convw_ref[j:j + 1, :] * xc_ref[5 + j:5 + j + tb, :]
        qk_m = _silu(conv)
        p_ref[:, C_MQ:C_MQ + 256] = qk_m[:, 0:256] * (64 ** -0.5)
        p_ref[:, C_MK:C_MK + 256] = qk_m[:, 256:512]
        xc_ref[0:8, :] = xc_ref[tb:tb + 8, :]
        p_ref[:, C_MV:C_MV + 512] = proj(C_MV, 512)
        p_ref[:, C_MO:C_MO + 512] = _sigmoid(proj(C_MO, 512))
        p_ref[:, C_HGZ:C_HGZ + 512] = _silu(proj(C_HGZ, 512)) * hgnw_ref[...]
        p_ref[:, C_GZ:C_GZ + 512] = _silu(proj(C_GZ, 512)) * glanw_ref[...]
        p_ref[:, C_MZ:C_MZ + 512] = _silu(proj(C_MZ, 512)) * mlnw_ref[...]

    zero = jnp.zeros((CHUNK, 128), BF16)
    lane128 = lax.broadcasted_iota(jnp.int32, (128, 128), 1)
    lane_lo = lax.broadcasted_iota(jnp.int32, (CHUNK, 128), 1) < 64

    def decay_factors(a_small):
        b = a_small[0:CHUNK, :]
        xb = jnp.exp2(b)
        xs = jnp.exp2(b[CHUNK - 1:CHUNK, :] - b)
        levels = []
        for lvl in range(3):
            s = CHUNK >> (lvl + 1)
            parts = []
            for i0 in range(0, CHUNK, 2 * s):
                ref = b[i0 + s - 1:i0 + s, :]
                parts.append(ref - b[i0:i0 + s, :])
                parts.append(b[i0 + s:i0 + 2 * s, :] - ref)
            levels.append(jnp.exp2(jnp.concatenate(parts, axis=0)))
        for lvl in range(3):
            levels.append(jnp.exp2(a_small[CHUNK * (lvl + 1):CHUNK * (lvl + 2), :]))
        return xb, xs, levels

    def gla_levels(q, k, levels, ncols):
        reps = ncols // 128
        zs = []
        for lvl in range(N_LEVELS):
            s = CHUNK >> (lvl + 1)
            if s >= 8:
                u = jnp.concatenate([(q if (i0 // s) % 2 else k)[i0:i0 + s, :] for i0 in range(0, CHUNK, s)], axis=0)
            else:
                qrow = jnp.concatenate([qrows_ref[lvl]] * reps, axis=1) > 0.5
                u = jnp.where(qrow, q, k)
            zs.append((u * levels[lvl]).astype(BF16))
        return zs, q.astype(BF16), k.astype(BF16)

    def pair_blockdiag(m):
        return jnp.concatenate([jnp.concatenate([m[:, 0:128], zero], axis=1),
                                jnp.concatenate([zero, m[:, 128:256]], axis=1)], axis=0)

    def halves_blockdiag(m):
        zeros = jnp.zeros_like(m)
        return jnp.concatenate([jnp.where(lane_lo, m, zeros), jnp.where(lane_lo, zeros, m)], axis=0)

    def packed_state_operand(st):
        sb = st.astype(BF16)
        zeros = jnp.zeros_like(sb)
        return jnp.concatenate([jnp.where(lane128 < 64, sb, zeros), jnp.where(lane128 < 64, zeros, sb)], axis=0)

    def packed_own(upd):
        return jnp.where(lane128 < 64, upd[0:128, :], upd[128:256, :])

    def stage_factors(c):
        rows = pl.ds(c * CHUNK, CHUNK)
        xb, xs, levels = decay_factors(_exact_lhs_dot(e2_ref[...], g_ref[rows, :]))
        d = dict(rows=rows, xlast=xb[CHUNK - 1:CHUNK, :])
        q_hg = p_ref[rows, C_HGQ:C_HGQ + 512]
        k_hg = p_ref[rows, C_HGF:C_HGF + 512]
        d["hg"] = gla_levels(q_hg, k_hg, [lv[:, 0:512] for lv in levels], 512)
        d["hg_q0"] = (q_hg * xb[:, 0:512]).astype(BF16)
        d["hg_khat"] = (k_hg * xs[:, 0:512]).astype(BF16)
        q_g = p_ref[rows, C_GQ:C_GQ + 256]
        k_g = p_ref[rows, C_GK:C_GK + 256]
        d["g"] = gla_levels(q_g, k_g, [lv[:, 512:768] for lv in levels], 256)
        d["g_q0"] = (q_g * xb[:, 512:768]).astype(BF16)
        d["g_khat"] = (k_g * xs[:, 512:768]).astype(BF16)
        return d

    def stage_mlstm_factors(d):
        rows = d["rows"]
        i_exp = mg_ref[rows, 0:256]
        b_exp = mg_ref[rows, 256:512]
        dm = b_exp - mg_ref[rows, 512:768] + negm_ref[...]
        m_loc = jnp.zeros((CHUNK, 256), F32)
        for hh in range(4):
            mh = jnp.max(dm[:, 64 * hh:64 * (hh + 1)], axis=-1, keepdims=True)
            m_loc = m_loc + mh * lanehead_ref[hh:hh + 1, :]
        b_last = b_exp[CHUNK - 1:CHUNK, :]
        a_upd = b_last - b_exp + i_exp
        mk = jnp.max(a_upd, axis=0, keepdims=True)
        d.update(b_exp=b_exp, b_last=b_last, m_loc=m_loc, mk=mk, w_loc=jnp.exp(dm - m_loc),
                 khat_m=jnp.exp(a_upd - mk) * p_ref[rows, C_MK:C_MK + 256])

    def stage_scores(d):
        rows = d["rows"]
        zs, qd, kd = d["hg"]
        v_hg_b = p_ref[rows, C_HGI:C_HGI + 512].astype(BF16)
        att_hg, u_hg = [], []
        for p in range(2):
            lanes = slice(256 * p, 256 * (p + 1))
            att = _dot(qd[:, lanes], pair_blockdiag(kd[:, lanes]), NT) * attmask_ref[N_LEVELS]
            for lvl in range(N_LEVELS):
                zp = zs[lvl][:, lanes]
                att = att + _dot(zp, pair_blockdiag(zp), NT) * attmask_ref[lvl]
            att_hg.append(att.astype(BF16))
            upd = _dot(v_hg_b[:, lanes], d["hg_khat"][:, lanes], TN)
            u_hg += [upd[0:128, 0:128], upd[128:256, 128:256]]
        zs, qd, kd = d["g"]
        v_g_b = p_ref[rows, C_GV:C_GV + 512].astype(BF16)
        att_g, u_g = [], []
        for p in range(2):
            lanes = slice(128 * p, 128 * (p + 1))
            att = _dot(qd[:, lanes], halves_blockdiag(kd[:, lanes]), NT) * attmask_ref[N_LEVELS]
            for lvl in range(N_LEVELS):
                zp = zs[lvl][:, lanes]
                att = att + _dot(zp, halves_blockdiag(zp), NT) * attmask_ref[lvl]
            att_g.append(att.astype(BF16))
            u_g.append(packed_own(_dot(v_g_b[:, 256 * p:256 * (p + 1)], d["g_khat"][:, lanes], TN)))
        q_m_b = p_ref[rows, C_MQ:C_MQ + 256].astype(BF16)
        k_m_b = p_ref[rows, C_MK:C_MK + 256].astype(BF16)
        v_m_b = p_ref[rows, C_MV:C_MV + 512].astype(BF16)
        khat_m_b = d["khat_m"].astype(BF16)
        s_m, u_m, rs_m = [], [], []
        for p in range(2):
            lanes = slice(128 * p, 128 * (p + 1))
            s_loc = _dot(q_m_b[:, lanes], halves_blockdiag(k_m_b[:, lanes]), NT) * d["w_loc"][:, lanes]
            s_m.append(s_loc.astype(BF16))
            u_m.append(packed_own(_dot(v_m_b[:, 256 * p:256 * (p + 1)], khat_m_b[:, lanes], TN)))
            rs_m += [jnp.sum(s_loc[:, 0:64], axis=-1, keepdims=True), jnp.sum(s_loc[:, 64:128], axis=-1, keepdims=True)]
        d.update(att_hg=att_hg, u_hg=u_hg, att_g=att_g, u_g=u_g, s_m=s_m, u_m=u_m, rs_m=rs_m,
                 q_m_b=q_m_b, v_hg_b=v_hg_b, v_g_b=v_g_b, v_m_b=v_m_b)

    def stage_values(d):
        d["o_hg"] = [_dot(d["att_hg"][p], pair_blockdiag(d["v_hg_b"][:, 256 * p:256 * (p + 1)])) for p in range(2)]
        d["o_g"] = [_dot(d["att_g"][p], pair_blockdiag(d["v_g_b"][:, 256 * p:256 * (p + 1)])) for p in range(2)]
        d["o_m"] = [_dot(d["s_m"][p], pair_blockdiag(d["v_m_b"][:, 256 * p:256 * (p + 1)])) for p in range(2)]

    def stage_state(d):
        rows = d["rows"]
        xlast = d["xlast"]
        o_heads = []
        for hh in range(4):
            hl = slice(128 * hh, 128 * (hh + 1))
            st = sthg_ref[hh]
            o_heads.append(d["o_hg"][hh // 2][:, 128 * (hh % 2):128 * (hh % 2 + 1)]
                           + _dot(d["hg_q0"][:, hl], st.astype(BF16), NT))
            sthg_ref[hh] = st * xlast[:, hl] + d["u_hg"][hh]
        y_ref[rows, 0:512] = (_head_norm(jnp.concatenate(o_heads, axis=1)) * p_ref[rows, C_HGZ:C_HGZ + 512]).astype(BF16)
        o_pairs = []
        for p in range(2):
            lanes = slice(128 * p, 128 * (p + 1))
            st = stg_ref[p]
            o_pairs.append(d["o_g"][p] + _dot(d["g_q0"][:, lanes], packed_state_operand(st), NT))
            stg_ref[p] = st * xlast[:, 512 + 128 * p:512 + 128 * (p + 1)] + d["u_g"][p]
        y_ref[rows, 512:1024] = (_head_norm(jnp.concatenate(o_pairs, axis=1)) * p_ref[rows, C_GZ:C_GZ + 512]).astype(BF16)
        b_exp, m_loc = d["b_exp"], d["m_loc"]
        m_prev = nm_ref[1:2, :]
        n_prev = nm_ref[0:1, :]
        m_t = jnp.maximum(b_exp + m_prev, m_loc)
        corr = jnp.exp(m_loc - m_t)
        inter = jnp.exp(b_exp + m_prev - m_t)
        m_new = m_t[CHUNK - 1:CHUNK, :]
        scale = jnp.exp(d["b_last"] + m_prev - m_new)
        upd_scale = jnp.exp(d["mk"] - m_new)
        qn = p_ref[rows, C_MQ:C_MQ + 256] * n_prev
        hs = []
        for p in range(2):
            lanes = slice(128 * p, 128 * (p + 1))
            ctm = ct_ref[p]
            num_inter = _dot(d["q_m_b"][:, lanes], packed_state_operand(ctm), NT)
            ct_ref[p] = ctm * scale[:, lanes] + d["u_m"][p] * upd_scale[:, lanes]
            for j in range(2):
                hh = 2 * p + j
                col = slice(64 * hh, 64 * hh + 1)
                den = (inter[:, col] * jnp.sum(qn[:, 64 * hh:64 * (hh + 1)], axis=-1, keepdims=True)
                       + corr[:, col] * d["rs_m"][hh])
                vs = slice(128 * j, 128 * (j + 1))
                num = inter[:, col] * num_inter[:, vs] + corr[:, col] * d["o_m"][p][:, vs]
                hs.append(num / jnp.maximum(jnp.abs(den), jnp.exp(-m_t[:, col])))
        h_m = jnp.concatenate(hs, axis=1) * p_ref[rows, C_MO:C_MO + 512]
        y_ref[rows, 1024:1536] = (_head_norm(h_m) * p_ref[rows, C_MZ:C_MZ + 512]).astype(BF16)
        nm_ref[0:1, :] = n_prev * scale + jnp.sum(d["khat_m"], axis=0, keepdims=True) * upd_scale
        nm_ref[1:2, :] = m_new

    chunks = [stage_factors(c) for c in range(n_chunks)]
    project_rest()
    for d in chunks:
        stage_mlstm_factors(d)
    for i in range(n_chunks + 2):
        if i < n_chunks:
            stage_scores(chunks[i])
        if 0 <= i - 1 < n_chunks:
            stage_values(chunks[i - 1])
        if 0 <= i - 2 < n_chunks:
            stage_state(chunks[i - 2])

    out = x + _dot(y_ref[...], wout_ref[...])
    if apply_final_norm:
        out = out * lax.rsqrt(jnp.mean(out * out, axis=-1, keepdims=True) + NORM_EPS) * fnw_ref[...]
    out_ref[0] = out


def _whole(arr):
    nd = arr.ndim
    return pl.BlockSpec(arr.shape, lambda b, t: (0,) * nd, pipeline_mode=pl.Buffered(1))


def _slab(arr, layer):
    nd = arr.ndim - 1
    return pl.BlockSpec((None,) + arr.shape[1:], lambda b, t: (layer,) + (0,) * nd, pipeline_mode=pl.Buffered(1))


def _layer_call(layer, apply_final_norm, x, stacked, shared, consts):
    bsz, seq, d = x.shape
    tb = TIME_BLOCK
    assert seq % tb == 0 and tb % CHUNK == 0 and d == D_MODEL
    const_arrays = [consts[k] for k in CONST_NAMES]
    in_specs = ([pl.BlockSpec((1, tb, d), lambda b, t: (b, t, 0))]
                + [_slab(a, layer) for a in stacked] + [_whole(a) for a in shared] + [_whole(a) for a in const_arrays])
    scratch = [
        pltpu.VMEM((tb, N_PROJ), F32),
        pltpu.VMEM((tb, 768), F32),
        pltpu.VMEM((tb, 768), F32),
        pltpu.VMEM((tb + 8, 512), F32),
        pltpu.VMEM((tb, 1536), BF16),
        pltpu.VMEM((4, 128, 128), F32),
        pltpu.VMEM((2, 128, 128), F32),
        pltpu.VMEM((2, 128, 128), F32),
        pltpu.VMEM((8, 256), F32),
    ]
    return pl.pallas_call(
        functools.partial(_layer_kernel, layer, apply_final_norm),
        grid=(bsz, seq // tb),
        in_specs=in_specs,
        out_specs=pl.BlockSpec((1, tb, d), lambda b, t: (b, t, 0)),
        out_shape=jax.ShapeDtypeStruct(x.shape, F32),
        scratch_shapes=scratch,
        compiler_params=pltpu.CompilerParams(dimension_semantics=("arbitrary", "arbitrary"),
                                             vmem_limit_bytes=VMEM_LIMIT_BYTES),
        name=f"trunk_layer{layer}",
    )(x, *stacked, *shared, *const_arrays)


SEGMENT_MOVES = (
    (0, 0, 3072), (3088, 3072, 1536), (4632, 4608, 1024), (3072, C_SMALL + S_GLA_A, 16), (4624, C_SMALL + S_ML_I, 8))


def _reorder_kernel(w_ref, o_ref):
    rows = w_ref.shape[0]
    for src, dst, width in SEGMENT_MOVES:
        o_ref[:, dst:dst + width] = w_ref[:, src:src + width].astype(o_ref.dtype)
    used = C_SMALL + S_ML_F + 4
    o_ref[:, used:N_PROJ] = jnp.zeros((rows, N_PROJ - used), o_ref.dtype)


def _reorder_columns(w, dtype, row_block):
    depth, rows, n_in = w.shape
    return pl.pallas_call(
        _reorder_kernel,
        grid=(depth, rows // row_block),
        in_specs=[pl.BlockSpec((None, row_block, n_in), lambda l, r: (l, r, 0))],
        out_specs=pl.BlockSpec((None, row_block, N_PROJ), lambda l, r: (l, r, 0)),
        out_shape=jax.ShapeDtypeStruct((depth, rows, N_PROJ), dtype),
        compiler_params=pltpu.CompilerParams(dimension_semantics=("arbitrary", "arbitrary"),
                                             vmem_limit_bytes=VMEM_LIMIT_BYTES),
        name="reorder_columns",
    )(w)


def kernel(x, norm_w, w_in, b_in, hg_lb_logits, hg_norm_w, gla_w_a2, gla_b_a2, gla_norm_w,
           ml_conv_w, ml_conv_b, ml_norm_w, w_out, final_norm_w):
    depth = w_in.shape[0]
    consts = _constants()
    wa2_pad = jnp.zeros((depth, 128, 256), F32).at[:, S_GLA_A:S_GLA_A + 16, :].set(gla_w_a2).astype(BF16)
    row = lambda v: v[:, None, :]
    stacked = (row(norm_w), _reorder_columns(w_in, BF16, 256), _reorder_columns(row(b_in), F32, 1), row(hg_norm_w),
               wa2_pad, row(gla_b_a2), row(gla_norm_w), ml_conv_w, row(ml_conv_b), row(ml_norm_w),
               w_out.astype(BF16))
    shared = (hg_lb_logits, final_norm_w[None, :])
    for l in range(depth):
        x = _layer_call(l, l == depth - 1, x, stacked, shared, consts)
    return x
```

```python
import functools
import math

import numpy as np
import jax
import jax.numpy as jnp
from jax import lax
from jax.experimental import pallas as pl
from jax.experimental.pallas import tpu as pltpu

D_MODEL = 1024
CHUNK = 64
NORM_EPS = 1e-6
N_LEVELS = 6
GLA_TAU = 16.0
LOG2E = math.log2(math.e)
TIME_BLOCK = 256
VMEM_LIMIT_BYTES = 56 * 1024 * 1024

C_HGQ, C_HGF, C_HGI, C_HGZ = 0, 512, 1024, 1536
C_GQ, C_GK, C_GV, C_GZ = 2048, 2304, 2560, 3072
C_MQ, C_MK, C_MV, C_MO, C_MZ = 3584, 3840, 4096, 4608, 5120
C_SMALL = 5632
N_PROJ = 5760
S_GLA_A, S_ML_I, S_ML_F = 0, 16, 20

F32 = jnp.float32
BF16 = jnp.bfloat16
NT = (((1,), (1,)), ((), ()))
TN = (((0,), (0,)), ((), ()))


def _dot(a, b, dims=None):
    if dims is None:
        return jnp.dot(a, b, preferred_element_type=F32)
    return lax.dot_general(a, b, dims, preferred_element_type=F32)


def _two_terms(x):
    hi = x.astype(BF16)
    return hi, (x - hi.astype(F32)).astype(BF16)


def _exact_lhs_dot(a2_bf16, x):
    return _dot(a2_bf16, jnp.concatenate(_two_terms(x), axis=0))


def _exact_rhs_dot(x, b2_bf16):
    return _dot(jnp.concatenate(_two_terms(x), axis=1), b2_bf16)


def _sigmoid(x):
    return 0.5 * jnp.tanh(0.5 * x) + 0.5


def _silu(x):
    half = 0.5 * x
    return half * jnp.tanh(half) + half


def _log_sigmoid(x):
    return jnp.minimum(x, 0.0) - jnp.log(1.0 + jnp.exp(-jnp.abs(x)))


def _constants():
    c = CHUNK
    t = np.arange(c)
    prefix = t[None, :] <= t[:, None]
    fine, qrow, att = [], [], []
    for lvl in range(1, N_LEVELS + 1):
        s = c >> lvl
        pos = t % (2 * s)
        ref = (t // (2 * s)) * 2 * s + s - 1
        lo = np.minimum(t, ref)[:, None]
        hi = np.maximum(t, ref)[:, None]
        if s < 8:
            fine.append((t[None, :] > lo) & (t[None, :] <= hi))
        qrow.append(pos >= s)
        same_pair = (t[:, None] // (2 * s)) == (t[None, :] // (2 * s))
        att.append(same_pair & (pos >= s)[:, None] & (pos < s)[None, :])
    att.append(np.eye(c, dtype=bool))
    e_mat = np.concatenate([prefix] + fine, axis=0).astype(np.float32)
    e2 = np.concatenate([e_mat, e_mat], axis=1)
    q_rows = np.repeat(np.stack(qrow)[:, :, None], 128, axis=2).astype(np.float32)
    att_mask = np.tile(np.stack(att).astype(np.float32), (1, 1, 2))
    lane_head = np.kron(np.eye(4), np.ones((1, 64))).astype(np.float32)
    seg = np.zeros((128, 512), np.float32)
    for h in range(4):
        seg[S_ML_I + h, 64 * h:64 * (h + 1)] = 1.0
        seg[S_ML_F + h, 256 + 64 * h:256 + 64 * (h + 1)] = 1.0
    seg2 = np.concatenate([seg, seg], axis=0)
    eye4 = np.tile(np.eye(c, dtype=np.float32), (1, 4))
    negm = np.tile(np.where(t[None, :] <= t[:, None], 0.0, -np.inf).astype(np.float32), (1, 4))
    return dict(e2=jnp.asarray(e2, BF16), q_rows=jnp.asarray(q_rows), att_mask=jnp.asarray(att_mask),
                lane_head=jnp.asarray(lane_head), seg2=jnp.asarray(seg2, BF16),
                eye4=jnp.asarray(eye4), negm=jnp.asarray(negm))


CONST_NAMES = ("e2", "q_rows", "att_mask", "lane_head", "seg2", "eye4", "negm")


def _head_norm(o):
    outs = []
    for h in range(4):
        oh = o[:, 128 * h:128 * (h + 1)]
        ms = jnp.mean(oh * oh, axis=-1, keepdims=True)
        outs.append(oh * lax.rsqrt(ms + NORM_EPS))
    return jnp.concatenate(outs, axis=1)


def _layer_kernel(layer, apply_final_norm,
                  x_ref, nw_ref, win_ref, bin_ref, hgnw_ref, wa2_ref, ba2_ref, glanw_ref,
                  convw_ref, convb_ref, mlnw_ref, wout_ref, lbl_ref, fnw_ref,
                  e2_ref, qrows_ref, attmask_ref, lanehead_ref, seg2_ref, eye4_ref, negm_ref,
                  out_ref,
                  p_ref, g_ref, mg_ref, xc_ref, y_ref, sthg_ref, stg_ref, ct_ref, nm_ref):
    tb = x_ref.shape[1]
    n_chunks = tb // CHUNK

    @pl.when(pl.program_id(1) == 0)
    def _():
        sthg_ref[...] = jnp.zeros_like(sthg_ref)
        stg_ref[...] = jnp.zeros_like(stg_ref)
        ct_ref[...] = jnp.zeros_like(ct_ref)
        nm_ref[...] = jnp.zeros_like(nm_ref)
        xc_ref[0:8, :] = jnp.zeros((8, 512), F32)

    lg = lbl_ref[...]
    ex = jnp.exp(lg - jnp.max(lg, axis=0, keepdims=True))
    lb = jnp.zeros((1, 512), F32)
    for i in range(1, layer + 1):
        lb = lb + ex[i:i + 1, :]
    lb = lb / jnp.sum(ex, axis=0, keepdims=True)

    x = x_ref[0]
    h = x * lax.rsqrt(jnp.mean(x * x, axis=-1, keepdims=True) + NORM_EPS) * nw_ref[...]
    hb = h.astype(BF16)

    def proj(c0, width):
        return _dot(hb, win_ref[:, c0:c0 + width]) + bin_ref[:, c0:c0 + width]

    small = proj(C_SMALL, 128)
    a = _dot(small.astype(BF16), wa2_ref[...]) + ba2_ref[...]
    g_ref[:, 512:768] = _log_sigmoid(a) * (LOG2E / GLA_TAU)
    lane = lax.broadcasted_iota(jnp.int32, (tb, 128), 1)
    small2 = jnp.where((lane >= S_ML_F) & (lane < S_ML_F + 4), _log_sigmoid(small), small)
    ifx = _exact_rhs_dot(small2, seg2_ref[...])
    mg_ref[:, 0:256] = ifx[:, 0:256]
    l2 = e2_ref[0:CHUNK, :]
    ones2 = jnp.ones((CHUNK, 2 * CHUNK), BF16)
    for c in range(n_chunks):
        rs = slice(c * CHUNK, (c + 1) * CHUNK)
        b_exp = _exact_lhs_dot(l2, ifx[rs, 256:512])
        mg_ref[rs, 256:512] = b_exp
        mg_ref[rs, 512:768] = _exact_lhs_dot(ones2, (b_exp - ifx[rs, 0:256]) * eye4_ref[...])

    f = lb + (1.0 - lb) * _sigmoid(proj(C_HGF, 512))
    p_ref[:, C_HGF:C_HGF + 512] = 1.0 - f
    g_ref[:, 0:512] = jnp.log(f) * LOG2E
    p_ref[:, C_HGQ:C_HGQ + 512] = _silu(proj(C_HGQ, 512))
    gqk = proj(C_GQ, 512)
    p_ref[:, C_GQ:C_GQ + 256] = gqk[:, 0:256] * (64 ** -0.5)
    p_ref[:, C_GK:C_GK + 256] = gqk[:, 256:512]

    def project_rest():
        p_ref[:, C_HGI:C_HGI + 512] = proj(C_HGI, 512)
        p_ref[:, C_GV:C_GV + 512] = proj(C_GV, 512)
        xc_ref[8:8 + tb, :] = proj(C_MQ, 512)
        conv = convb_ref[...] + convw_ref[3:4, :] * xc_ref[8:8 + tb, :]
        for j in range(3):
            conv = conv + convw_ref[j:j + 1, :] * xc_ref[5 + j:5 + j + tb, :]
        qk_m = _silu(conv)
        p_ref[:, C_MQ:C_MQ + 256] = qk_m[:, 0:256] * (64 ** -0.5)
        p_ref[:, C_MK:C_MK + 256] = qk_m[:, 256:512]
        xc_ref[0:8, :] = xc_ref[tb:tb + 8, :]
        p_ref[:, C_MV:C_MV + 512] = proj(C_MV, 512)
        p_ref[:, C_MO:C_MO + 512] = _sigmoid(proj(C_MO, 512))
        p_ref[:, C_HGZ:C_HGZ + 512] = _silu(proj(C_HGZ, 512)) * hgnw_ref[...]
        p_ref[:, C_GZ:C_GZ + 512] = _silu(proj(C_GZ, 512)) * glanw_ref[...]
        p_ref[:, C_MZ:C_MZ + 512] = _silu(proj(C_MZ, 512)) * mlnw_ref[...]

    zero = jnp.zeros((CHUNK, 128), BF16)
    lane128 = lax.broadcasted_iota(jnp.int32, (128, 128), 1)
    lane_lo = lax.broadcasted_iota(jnp.int32, (CHUNK, 128), 1) < 64

    def decay_factors(a_small):
        b = a_small[0:CHUNK, :]
        xb = jnp.exp2(b)
        xs = jnp.exp2(b[CHUNK - 1:CHUNK, :] - b)
        levels = []
        for lvl in range(3):
            s = CHUNK >> (lvl + 1)
            parts = []
            for i0 in range(0, CHUNK, 2 * s):
                ref = b[i0 + s - 1:i0 + s, :]
                parts.append(ref - b[i0:i0 + s, :])
                parts.append(b[i0 + s:i0 + 2 * s, :] - ref)
            levels.append(jnp.exp2(jnp.concatenate(parts, axis=0)))
        for lvl in range(3):
            levels.append(jnp.exp2(a_small[CHUNK * (lvl + 1):CHUNK * (lvl + 2), :]))
        return xb, xs, levels

    def gla_levels(q, k, levels, ncols):
        reps = ncols // 128
        zs = []
        for lvl in range(N_LEVELS):
            s = CHUNK >> (lvl + 1)
            if s >= 8:
                u = jnp.concatenate([(q if (i0 // s) % 2 else k)[i0:i0 + s, :] for i0 in range(0, CHUNK, s)], axis=0)
            else:
                qrow = jnp.concatenate([qrows_ref[lvl]] * reps, axis=1) > 0.5
                u = jnp.where(qrow, q, k)
            zs.append((u * levels[lvl]).astype(BF16))
        return zs, q.astype(BF16), k.astype(BF16)

    def pair_blockdiag(m):
        return jnp.concatenate([jnp.concatenate([m[:, 0:128], zero], axis=1),
                                jnp.concatenate([zero, m[:, 128:256]], axis=1)], axis=0)

    def halves_blockdiag(m):
        zeros = jnp.zeros_like(m)
        return jnp.concatenate([jnp.where(lane_lo, m, zeros), jnp.where(lane_lo, zeros, m)], axis=0)

    def packed_state_operand(st):
        sb = st.astype(BF16)
        zeros = jnp.zeros_like(sb)
        return jnp.concatenate([jnp.where(lane128 < 64, sb, zeros), jnp.where(lane128 < 64, zeros, sb)], axis=0)

    def packed_own(upd):
        return jnp.where(lane128 < 64, upd[0:128, :], upd[128:256, :])

    def stage_factors(c):
        rows = pl.ds(c * CHUNK, CHUNK)
        xb, xs, levels = decay_factors(_exact_lhs_dot(e2_ref[...], g_ref[rows, :]))
        d = dict(rows=rows, xlast=xb[CHUNK - 1:CHUNK, :])
        q_hg = p_ref[rows, C_HGQ:C_HGQ + 512]
        k_hg = p_ref[rows, C_HGF:C_HGF + 512]
        d["hg"] = gla_levels(q_hg, k_hg, [lv[:, 0:512] for lv in levels], 512)
        d["hg_q0"] = (q_hg * xb[:, 0:512]).astype(BF16)
        d["hg_khat"] = (k_hg * xs[:, 0:512]).astype(BF16)
        q_g = p_ref[rows, C_GQ:C_GQ + 256]
        k_g = p_ref[rows, C_GK:C_GK + 256]
        d["g"] = gla_levels(q_g, k_g, [lv[:, 512:768] for lv in levels], 256)
        d["g_q0"] = (q_g * xb[:, 512:768]).astype(BF16)
        d["g_khat"] = (k_g * xs[:, 512:768]).astype(BF16)
        return d

    def stage_mlstm_factors(d):
        rows = d["rows"]
        i_exp = mg_ref[rows, 0:256]
        b_exp = mg_ref[rows, 256:512]
        dm = b_exp - mg_ref[rows, 512:768] + negm_ref[...]
        m_loc = jnp.zeros((CHUNK, 256), F32)
        for hh in range(4):
            mh = jnp.max(dm[:, 64 * hh:64 * (hh + 1)], axis=-1, keepdims=True)
            m_loc = m_loc + mh * lanehead_ref[hh:hh + 1, :]
        b_last = b_exp[CHUNK - 1:CHUNK, :]
        a_upd = b_last - b_exp + i_exp
        mk = jnp.max(a_upd, axis=0, keepdims=True)
        d.update(b_exp=b_exp, b_last=b_last, m_loc=m_loc, mk=mk, w_loc=jnp.exp(dm - m_loc),
                 khat_m=jnp.exp(a_upd - mk) * p_ref[rows, C_MK:C_MK + 256])

    def stage_scores(d):
        rows = d["rows"]
        zs, qd, kd = d["hg"]
        v_hg_b = p_ref[rows, C_HGI:C_HGI + 512].astype(BF16)
        att_hg, u_hg = [], []
        for p in range(2):
            lanes = slice(256 * p, 256 * (p + 1))
            att = _dot(qd[:, lanes], pair_blockdiag(kd[:, lanes]), NT) * attmask_ref[N_LEVELS]
            for lvl in range(N_LEVELS):
                zp = zs[lvl][:, lanes]
                att = att + _dot(zp, pair_blockdiag(zp), NT) * attmask_ref[lvl]
            att_hg.append(att.astype(BF16))
            upd = _dot(v_hg_b[:, lanes], d["hg_khat"][:, lanes], TN)
            u_hg += [upd[0:128, 0:128], upd[128:256, 128:256]]
        zs, qd, kd = d["g"]
        v_g_b = p_ref[rows, C_GV:C_GV + 512].astype(BF16)
        att_g, u_g = [], []
        for p in range(2):
            lanes = slice(128 * p, 128 * (p + 1))
            att = _dot(qd[:, lanes], halves_blockdiag(kd[:, lanes]), NT) * attmask_ref[N_LEVELS]
            for lvl in range(N_LEVELS):
                zp = zs[lvl][:, lanes]
                att = att + _dot(zp, halves_blockdiag(zp), NT) * attmask_ref[lvl]
            att_g.append(att.astype(BF16))
            u_g.append(packed_own(_dot(v_g_b[:, 256 * p:256 * (p + 1)], d["g_khat"][:, lanes], TN)))
        q_m_b = p_ref[rows, C_MQ:C_MQ + 256].astype(BF16)
        k_m_b = p_ref[rows, C_MK:C_MK + 256].astype(BF16)
        v_m_b = p_ref[rows, C_MV:C_MV + 512].astype(BF16)
        khat_m_b = d["khat_m"].astype(BF16)
        s_m, u_m, rs_m = [], [], []
        for p in range(2):
            lanes = slice(128 * p, 128 * (p + 1))
            s_loc = _dot(q_m_b[:, lanes], halves_blockdiag(k_m_b[:, lanes]), NT) * d["w_loc"][:, lanes]
            s_m.append(s_loc.astype(BF16))
            u_m.append(packed_own(_dot(v_m_b[:, 256 * p:256 * (p + 1)], khat_m_b[:, lanes], TN)))
            rs_m += [jnp.sum(s_loc[:, 0:64], axis=-1, keepdims=True), jnp.sum(s_loc[:, 64:128], axis=-1, keepdims=True)]
        d.update(att_hg=att_hg, u_hg=u_hg, att_g=att_g, u_g=u_g, s_m=s_m, u_m=u_m, rs_m=rs_m,
                 q_m_b=q_m_b, v_hg_b=v_hg_b, v_g_b=v_g_b, v_m_b=v_m_b)

    def stage_values(d):
        d["o_hg"] = [_dot(d["att_hg"][p], pair_blockdiag(d["v_hg_b"][:, 256 * p:256 * (p + 1)])) for p in range(2)]
        d["o_g"] = [_dot(d["att_g"][p], pair_blockdiag(d["v_g_b"][:, 256 * p:256 * (p + 1)])) for p in range(2)]
        d["o_m"] = [_dot(d["s_m"][p], pair_blockdiag(d["v_m_b"][:, 256 * p:256 * (p + 1)])) for p in range(2)]

    def stage_state(d):
        rows = d["rows"]
        xlast = d["xlast"]
        o_heads = []
        for hh in range(4):
            hl = slice(128 * hh, 128 * (hh + 1))
            st = sthg_ref[hh]
            o_heads.append(d["o_hg"][hh // 2][:, 128 * (hh % 2):128 * (hh % 2 + 1)]
                           + _dot(d["hg_q0"][:, hl], st.astype(BF16), NT))
            sthg_ref[hh] = st * xlast[:, hl] + d["u_hg"][hh]
        y_ref[rows, 0:512] = (_head_norm(jnp.concatenate(o_heads, axis=1)) * p_ref[rows, C_HGZ:C_HGZ + 512]).astype(BF16)
        o_pairs = []
        for p in range(2):
            lanes = slice(128 * p, 128 * (p + 1))
            st = stg_ref[p]
            o_pairs.append(d["o_g"][p] + _dot(d["g_q0"][:, lanes], packed_state_operand(st), NT))
            stg_ref[p] = st * xlast[:, 512 + 128 * p:512 + 128 * (p + 1)] + d["u_g"][p]
        y_ref[rows, 512:1024] = (_head_norm(jnp.concatenate(o_pairs, axis=1)) * p_ref[rows, C_GZ:C_GZ + 512]).astype(BF16)
        b_exp, m_loc = d["b_exp"], d["m_loc"]
        m_prev = nm_ref[1:2, :]
        n_prev = nm_ref[0:1, :]
        m_t = jnp.maximum(b_exp + m_prev, m_loc)
        corr = jnp.exp(m_loc - m_t)
        inter = jnp.exp(b_exp + m_prev - m_t)
        m_new = m_t[CHUNK - 1:CHUNK, :]
        scale = jnp.exp(d["b_last"] + m_prev - m_new)
        upd_scale = jnp.exp(d["mk"] - m_new)
        qn = p_ref[rows, C_MQ:C_MQ + 256] * n_prev
        hs = []
        for p in range(2):
            lanes = slice(128 * p, 128 * (p + 1))
            ctm = ct_ref[p]
            num_inter = _dot(d["q_m_b"][:, lanes], packed_state_operand(ctm), NT)
            ct_ref[p] = ctm * scale[:, lanes] + d["u_m"][p] * upd_scale[:, lanes]
            for j in range(2):
                hh = 2 * p + j
                col = slice(64 * hh, 64 * hh + 1)
                den = (inter[:, col] * jnp.sum(qn[:, 64 * hh:64 * (hh + 1)], axis=-1, keepdims=True)
                       + corr[:, col] * d["rs_m"][hh])
                vs = slice(128 * j, 128 * (j + 1))
                num = inter[:, col] * num_inter[:, vs] + corr[:, col] * d["o_m"][p][:, vs]
                hs.append(num / jnp.maximum(jnp.abs(den), jnp.exp(-m_t[:, col])))
        h_m = jnp.concatenate(hs, axis=1) * p_ref[rows, C_MO:C_MO + 512]
        y_ref[rows, 1024:1536] = (_head_norm(h_m) * p_ref[rows, C_MZ:C_MZ + 512]).astype(BF16)
        nm_ref[0:1, :] = n_prev * scale + jnp.sum(d["khat_m"], axis=0, keepdims=True) * upd_scale
        nm_ref[1:2, :] = m_new

    chunks = [stage_factors(c) for c in range(n_chunks)]
    project_rest()
    for d in chunks:
        stage_mlstm_factors(d)
    for i in range(n_chunks + 2):
        if i < n_chunks:
            stage_scores(chunks[i])
        if 0 <= i - 1 < n_chunks:
            stage_values(chunks[i - 1])
        if 0 <= i - 2 < n_chunks:
            stage_state(chunks[i - 2])

    out = x + _dot(y_ref[...], wout_ref[...])
    if apply_final_norm:
        out = out * lax.rsqrt(jnp.mean(out * out, axis=-1, keepdims=True) + NORM_EPS) * fnw_ref[...]
    out_ref[0] = out


def _whole(arr):
    nd = arr.ndim
    return pl.BlockSpec(arr.shape, lambda b, t: (0,) * nd, pipeline_mode=pl.Buffered(1))


def _slab(arr, layer):
    nd = arr.ndim - 1
    return pl.BlockSpec((None,) + arr.shape[1:], lambda b, t: (layer,) + (0,) * nd, pipeline_mode=pl.Buffered(1))


def _layer_call(layer, apply_final_norm, x, stacked, shared, consts):
    bsz, seq, d = x.shape
    tb = TIME_BLOCK
    assert seq % tb == 0 and tb % CHUNK == 0 and d == D_MODEL
    const_arrays = [consts[k] for k in CONST_NAMES]
    in_specs = ([pl.BlockSpec((1, tb, d), lambda b, t: (b, t, 0))]
                + [_slab(a, layer) for a in stacked] + [_whole(a) for a in shared] + [_whole(a) for a in const_arrays])
    scratch = [
        pltpu.VMEM((tb, N_PROJ), F32),
        pltpu.VMEM((tb, 768), F32),
        pltpu.VMEM((tb, 768), F32),
        pltpu.VMEM((tb + 8, 512), F32),
        pltpu.VMEM((tb, 1536), BF16),
        pltpu.VMEM((4, 128, 128), F32),
        pltpu.VMEM((2, 128, 128), F32),
        pltpu.VMEM((2, 128, 128), F32),
        pltpu.VMEM((8, 256), F32),
    ]
    return pl.pallas_call(
        functools.partial(_layer_kernel, layer, apply_final_norm),
        grid=(bsz, seq // tb),
        in_specs=in_specs,
        out_specs=pl.BlockSpec((1, tb, d), lambda b, t: (b, t, 0)),
        out_shape=jax.ShapeDtypeStruct(x.shape, F32),
        scratch_shapes=scratch,
        compiler_params=pltpu.CompilerParams(dimension_semantics=("arbitrary", "arbitrary"),
                                             vmem_limit_bytes=VMEM_LIMIT_BYTES),
        name=f"trunk_layer{layer}",
    )(x, *stacked, *shared, *const_arrays)


SEGMENT_MOVES = (
    (0, 0, 3072), (3088, 3072, 1536), (4632, 4608, 1024), (3072, C_SMALL + S_GLA_A, 16), (4624, C_SMALL + S_ML_I, 8))


TRANSPOSE_TILE = 256


def _reorder_bias(b):
    pad = jnp.zeros(b.shape[:-1] + (N_PROJ - (C_SMALL + S_ML_F + 4),), b.dtype)
    return jnp.concatenate([b[..., src:src + width] for src, _, width in SEGMENT_MOVES] + [pad], axis=-1)


def _weight_prep_kernel(wt_ref, o_ref):
    d_model = wt_ref.shape[1]
    for src, dst, width in SEGMENT_MOVES:
        for off in range(0, width, TRANSPOSE_TILE):
            n = min(TRANSPOSE_TILE, width - off)
            rows = wt_ref[src + off:src + off + n, :]
            if n < 128:
                rows = jnp.concatenate([rows, jnp.zeros((128 - n, d_model), rows.dtype)], axis=0)
            o_ref[:, dst + off:dst + off + n] = rows.T[:, 0:n].astype(o_ref.dtype)
    used = C_SMALL + S_ML_F + 4
    o_ref[:, used:N_PROJ] = jnp.zeros((d_model, N_PROJ - used), o_ref.dtype)


def _prepare_w_in(w_in):
    depth, d_model, n_in = w_in.shape
    return pl.pallas_call(
        _weight_prep_kernel,
        grid=(depth,),
        in_specs=[pl.BlockSpec((None, n_in, d_model), lambda l: (l, 0, 0), pipeline_mode=pl.Buffered(1))],
        out_specs=pl.BlockSpec((None, d_model, N_PROJ), lambda l: (l, 0, 0)),
        out_shape=jax.ShapeDtypeStruct((depth, d_model, N_PROJ), BF16),
        compiler_params=pltpu.CompilerParams(dimension_semantics=("arbitrary",), vmem_limit_bytes=VMEM_LIMIT_BYTES),
        name="prepare_w_in",
    )(jnp.transpose(w_in, (0, 2, 1)))


def kernel(x, norm_w, w_in, b_in, hg_lb_logits, hg_norm_w, gla_w_a2, gla_b_a2, gla_norm_w,
           ml_conv_w, ml_conv_b, ml_norm_w, w_out, final_norm_w):
    depth = w_in.shape[0]
    consts = _constants()
    wa2_pad = jnp.zeros((depth, 128, 256), F32).at[:, S_GLA_A:S_GLA_A + 16, :].set(gla_w_a2).astype(BF16)
    row = lambda v: v[:, None, :]
    stacked = (row(norm_w), _prepare_w_in(w_in), row(_reorder_bias(b_in)), row(hg_norm_w),
               wa2_pad, row(gla_b_a2), row(gla_norm_w), ml_conv_w, row(ml_conv_b), row(ml_norm_w),
               w_out.astype(BF16))
    shared = (hg_lb_logits, final_norm_w[None, :])
    for l in range(depth):
        x = _layer_call(l, l == depth - 1, x, stacked, shared, consts)
    return x
```

```python
import functools
import math

import numpy as np
import jax
import jax.numpy as jnp
from jax import lax
from jax.experimental import pallas as pl
from jax.experimental.pallas import tpu as pltpu

D_MODEL = 1024
CHUNK = 64
NORM_EPS = 1e-6
N_LEVELS = 6
GLA_TAU = 16.0
LOG2E = math.log2(math.e)
TIME_BLOCK = 512
VMEM_LIMIT_BYTES = 60 * 1024 * 1024

C_HGQ, C_HGF, C_HGI, C_HGZ = 0, 512, 1024, 1536
C_GQ, C_GK, C_GV, C_GZ = 2048, 2304, 2560, 3072
C_MQ, C_MK, C_MV, C_MO, C_MZ = 3584, 3840, 4096, 4608, 5120
C_SMALL = 5632
N_PROJ = 5760
S_GLA_A, S_ML_I, S_ML_F = 0, 16, 20

F32 = jnp.float32
BF16 = jnp.bfloat16
NT = (((1,), (1,)), ((), ()))
TN = (((0,), (0,)), ((), ()))


def _dot(a, b, dims=None):
    if dims is None:
        return jnp.dot(a, b, preferred_element_type=F32)
    return lax.dot_general(a, b, dims, preferred_element_type=F32)


def _two_terms(x):
    hi = x.astype(BF16)
    return hi, (x - hi.astype(F32)).astype(BF16)


def _exact_lhs_dot(a2_bf16, x):
    return _dot(a2_bf16, jnp.concatenate(_two_terms(x), axis=0))


def _exact_rhs_dot(x, b2_bf16):
    return _dot(jnp.concatenate(_two_terms(x), axis=1), b2_bf16)


def _sigmoid(x):
    return 0.5 * jnp.tanh(0.5 * x) + 0.5


def _sigmoid_for_log(x):
    return 1.0 / (1.0 + jnp.exp(-x))


def _silu(x):
    half = 0.5 * x
    return half * jnp.tanh(half) + half


def _log_sigmoid(x):
    return jnp.minimum(x, 0.0) - jnp.log(1.0 + jnp.exp(-jnp.abs(x)))


def _constants():
    c = CHUNK
    t = np.arange(c)
    prefix = t[None, :] <= t[:, None]
    fine, qrow, att = [], [], []
    for lvl in range(1, N_LEVELS + 1):
        s = c >> lvl
        pos = t % (2 * s)
        ref = (t // (2 * s)) * 2 * s + s - 1
        lo = np.minimum(t, ref)[:, None]
        hi = np.maximum(t, ref)[:, None]
        if s < 8:
            fine.append((t[None, :] > lo) & (t[None, :] <= hi))
        qrow.append(pos >= s)
        same_pair = (t[:, None] // (2 * s)) == (t[None, :] // (2 * s))
        att.append(same_pair & (pos >= s)[:, None] & (pos < s)[None, :])
    att.append(np.eye(c, dtype=bool))
    e_mat = np.concatenate([prefix] + fine, axis=0).astype(np.float32)
    e2 = np.concatenate([e_mat, e_mat], axis=1)
    q_rows = np.repeat(np.stack(qrow)[:, :, None], 128, axis=2).astype(np.float32)
    att_mask = np.tile(np.stack(att).astype(np.float32), (1, 1, 2))
    lane_head = np.kron(np.eye(4), np.ones((1, 64))).astype(np.float32)
    seg = np.zeros((128, 512), np.float32)
    for h in range(4):
        seg[S_ML_I + h, 64 * h:64 * (h + 1)] = 1.0
        seg[S_ML_F + h, 256 + 64 * h:256 + 64 * (h + 1)] = 1.0
    seg2 = np.concatenate([seg, seg], axis=0)
    eye4 = np.tile(np.eye(c, dtype=np.float32), (1, 4))
    negm = np.tile(np.where(t[None, :] <= t[:, None], 0.0, -np.inf).astype(np.float32), (1, 4))
    return dict(e2=jnp.asarray(e2, BF16), q_rows=jnp.asarray(q_rows), att_mask=jnp.asarray(att_mask),
                lane_head=jnp.asarray(lane_head), seg2=jnp.asarray(seg2, BF16),
                eye4=jnp.asarray(eye4), negm=jnp.asarray(negm))


CONST_NAMES = ("e2", "q_rows", "att_mask", "lane_head", "seg2", "eye4", "negm")


def _head_norm(o):
    outs = []
    for h in range(4):
        oh = o[:, 128 * h:128 * (h + 1)]
        ms = jnp.mean(oh * oh, axis=-1, keepdims=True)
        outs.append(oh * lax.rsqrt(ms + NORM_EPS))
    return jnp.concatenate(outs, axis=1)


def _layer_kernel(layer, apply_final_norm,
                  x_ref, nw_ref, win_ref, bin_ref, hgnw_ref, wa2_ref, ba2_ref, glanw_ref,
                  convw_ref, convb_ref, mlnw_ref, wout_ref, lbl_ref, fnw_ref,
                  e2_ref, qrows_ref, attmask_ref, lanehead_ref, seg2_ref, eye4_ref, negm_ref,
                  out_ref,
                  p_ref, v_ref, g_ref, mg_ref, xc_ref, y_ref, sthg_ref, stg_ref, ct_ref, nm_ref):
    tb = x_ref.shape[1]
    n_chunks = tb // CHUNK

    @pl.when(pl.program_id(1) == 0)
    def _():
        sthg_ref[...] = jnp.zeros_like(sthg_ref)
        stg_ref[...] = jnp.zeros_like(stg_ref)
        ct_ref[...] = jnp.zeros_like(ct_ref)
        nm_ref[...] = jnp.zeros_like(nm_ref)
        xc_ref[0:8, :] = jnp.zeros((8, 512), F32)

    lg = lbl_ref[...]
    ex = jnp.exp(lg - jnp.max(lg, axis=0, keepdims=True))
    lb = jnp.zeros((1, 512), F32)
    for i in range(1, layer + 1):
        lb = lb + ex[i:i + 1, :]
    lb = lb / jnp.sum(ex, axis=0, keepdims=True)

    x = x_ref[0]
    h = x * lax.rsqrt(jnp.mean(x * x, axis=-1, keepdims=True) + NORM_EPS) * nw_ref[...]
    hb = h.astype(BF16)

    def proj(c0, width):
        return _dot(hb, win_ref[:, c0:c0 + width]) + bin_ref[:, c0:c0 + width]

    small = proj(C_SMALL, 128)
    f = lb + (1.0 - lb) * _sigmoid_for_log(proj(C_HGF, 512))
    p_ref[:, C_HGF:C_HGF + 512] = 1.0 - f
    g_ref[:, 0:512] = jnp.log(f) * LOG2E
    a = _dot(small.astype(BF16), wa2_ref[...]) + ba2_ref[...]
    lane = lax.broadcasted_iota(jnp.int32, (tb, 128), 1)
    small2 = jnp.where((lane >= S_ML_F) & (lane < S_ML_F + 4), _log_sigmoid(small), small)
    ifx = _exact_rhs_dot(small2, seg2_ref[...])
    p_ref[:, C_HGQ:C_HGQ + 512] = _silu(proj(C_HGQ, 512))
    g_ref[:, 512:768] = _log_sigmoid(a) * (LOG2E / GLA_TAU)
    mg_ref[:, 0:256] = ifx[:, 0:256]
    l2 = e2_ref[0:CHUNK, :]
    ones2 = jnp.ones((CHUNK, 2 * CHUNK), BF16)
    b_exps = []
    for c in range(n_chunks):
        rs = slice(c * CHUNK, (c + 1) * CHUNK)
        b_exps.append(_exact_lhs_dot(l2, ifx[rs, 256:512]))
        mg_ref[rs, 256:512] = b_exps[c]
    gqk = proj(C_GQ, 512)
    p_ref[:, C_GQ:C_GQ + 256] = gqk[:, 0:256] * (64 ** -0.5)
    p_ref[:, C_GK:C_GK + 256] = gqk[:, 256:512]
    for c in range(n_chunks):
        rs = slice(c * CHUNK, (c + 1) * CHUNK)
        mg_ref[rs, 512:768] = _exact_lhs_dot(ones2, (b_exps[c] - ifx[rs, 0:256]) * eye4_ref[...])

    def project_rest():
        v_ref[:, 0:512] = proj(C_HGI, 512).astype(BF16)
        v_ref[:, 512:1024] = proj(C_GV, 512).astype(BF16)
        xc_ref[8:8 + tb, :] = proj(C_MQ, 512)
        conv = convb_ref[...] + convw_ref[3:4, :] * xc_ref[8:8 + tb, :]
        for j in range(3):
            conv = conv + convw_ref[j:j + 1, :] * xc_ref[5 + j:5 + j + tb, :]
        qk_m = _silu(conv)
        p_ref[:, C_MQ:C_MQ + 256] = qk_m[:, 0:256] * (64 ** -0.5)
        p_ref[:, C_MK:C_MK + 256] = qk_m[:, 256:512]
        xc_ref[0:8, :] = xc_ref[tb:tb + 8, :]
        v_ref[:, 1024:1536] = proj(C_MV, 512).astype(BF16)
        p_ref[:, C_MO:C_MO + 512] = _sigmoid(proj(C_MO, 512))
        p_ref[:, C_HGZ:C_HGZ + 512] = _silu(proj(C_HGZ, 512)) * hgnw_ref[...]
        p_ref[:, C_GZ:C_GZ + 512] = _silu(proj(C_GZ, 512)) * glanw_ref[...]
        p_ref[:, C_MZ:C_MZ + 512] = _silu(proj(C_MZ, 512)) * mlnw_ref[...]

    zero = jnp.zeros((CHUNK, 128), BF16)
    lane128 = lax.broadcasted_iota(jnp.int32, (128, 128), 1)
    lane_lo = lax.broadcasted_iota(jnp.int32, (CHUNK, 128), 1) < 64

    def decay_factors(a_small):
        b = a_small[0:CHUNK, :]
        xb = jnp.exp2(b)
        xs = jnp.exp2(b[CHUNK - 1:CHUNK, :] - b)
        levels = []
        for lvl in range(3):
            s = CHUNK >> (lvl + 1)
            parts = []
            for i0 in range(0, CHUNK, 2 * s):
                ref = b[i0 + s - 1:i0 + s, :]
                parts.append(ref - b[i0:i0 + s, :])
                parts.append(b[i0 + s:i0 + 2 * s, :] - ref)
            levels.append(jnp.exp2(jnp.concatenate(parts, axis=0)))
        for lvl in range(3):
            levels.append(jnp.exp2(a_small[CHUNK * (lvl + 1):CHUNK * (lvl + 2), :]))
        return xb, xs, levels

    def pair_keys_transposed(m):
        zf = jnp.zeros((CHUNK, 128), F32)
        bd = jnp.concatenate([jnp.concatenate([m[:, 0:128], zf], axis=1),
                              jnp.concatenate([zf, m[:, 128:256]], axis=1)], axis=0)
        return bd.T.astype(BF16)

    def gla_levels(q, k, levels, ncols, keys_t):
        reps = ncols // 128
        zs, zts = [], []
        for lvl in range(N_LEVELS):
            s = CHUNK >> (lvl + 1)
            if s >= 8:
                u = jnp.concatenate([(q if (i0 // s) % 2 else k)[i0:i0 + s, :] for i0 in range(0, CHUNK, s)], axis=0)
            else:
                qrow = jnp.concatenate([qrows_ref[lvl]] * reps, axis=1) > 0.5
                u = jnp.where(qrow, q, k)
            z = u * levels[lvl]
            zs.append(z.astype(BF16))
            zts.append(keys_t(z))
        zts.append(keys_t(k))
        return zs, q.astype(BF16), zts

    def pair_blockdiag(m):
        return jnp.concatenate([jnp.concatenate([m[:, 0:128], zero], axis=1),
                                jnp.concatenate([zero, m[:, 128:256]], axis=1)], axis=0)

    def halves_blockdiag(m):
        zeros = jnp.zeros_like(m)
        return jnp.concatenate([jnp.where(lane_lo, m, zeros), jnp.where(lane_lo, zeros, m)], axis=0)

    def packed_state_operand(st):
        sb = st.astype(BF16)
        zeros = jnp.zeros_like(sb)
        return jnp.concatenate([jnp.where(lane128 < 64, sb, zeros), jnp.where(lane128 < 64, zeros, sb)], axis=0)

    def packed_own(upd):
        return jnp.where(lane128 < 64, upd[0:128, :], upd[128:256, :])

    def stage_factors(c):
        rows = pl.ds(c * CHUNK, CHUNK)
        xb, xs, levels = decay_factors(_exact_lhs_dot(e2_ref[...], g_ref[rows, :]))
        d = dict(rows=rows, xlast=xb[CHUNK - 1:CHUNK, :])
        q_hg = p_ref[rows, C_HGQ:C_HGQ + 512]
        k_hg = p_ref[rows, C_HGF:C_HGF + 512]
        d["hg"] = gla_levels(q_hg, k_hg, [lv[:, 0:512] for lv in levels], 512,
                              lambda m: [pair_keys_transposed(m[:, 256 * p:256 * (p + 1)]) for p in range(2)])
        d["hg_q0"] = (q_hg * xb[:, 0:512]).astype(BF16)
        d["hg_khat"] = (k_hg * xs[:, 0:512]).astype(BF16)
        q_g = p_ref[rows, C_GQ:C_GQ + 256]
        k_g = p_ref[rows, C_GK:C_GK + 256]
        d["g"] = gla_levels(q_g, k_g, [lv[:, 512:768] for lv in levels], 256,
                             lambda m: [halves_blockdiag(m[:, 128 * p:128 * (p + 1)].astype(BF16)) for p in range(2)])
        d["g_q0"] = (q_g * xb[:, 512:768]).astype(BF16)
        d["g_khat"] = (k_g * xs[:, 512:768]).astype(BF16)
        return d

    def stage_mlstm_factors(d):
        rows = d["rows"]
        i_exp = mg_ref[rows, 0:256]
        b_exp = mg_ref[rows, 256:512]
        dm = b_exp - mg_ref[rows, 512:768] + negm_ref[...]
        m_loc = jnp.zeros((CHUNK, 256), F32)
        for hh in range(4):
            mh = jnp.max(dm[:, 64 * hh:64 * (hh + 1)], axis=-1, keepdims=True)
            m_loc = m_loc + mh * lanehead_ref[hh:hh + 1, :]
        b_last = b_exp[CHUNK - 1:CHUNK, :]
        a_upd = b_last - b_exp + i_exp
        mk = jnp.max(a_upd, axis=0, keepdims=True)
        d.update(b_exp=b_exp, b_last=b_last, m_loc=m_loc, mk=mk, w_loc=jnp.exp(dm - m_loc),
                 khat_m=jnp.exp(a_upd - mk) * p_ref[rows, C_MK:C_MK + 256])

    def stage_scores(d):
        rows = d["rows"]
        zs, qd, zts = d["hg"]
        v_hg_b = v_ref[rows, 0:512]
        att_hg, u_hg = [], []
        for p in range(2):
            lanes = slice(256 * p, 256 * (p + 1))
            att = _dot(qd[:, lanes], zts[N_LEVELS][p]) * attmask_ref[N_LEVELS]
            for lvl in range(N_LEVELS):
                att = att + _dot(zs[lvl][:, lanes], zts[lvl][p]) * attmask_ref[lvl]
            att_hg.append(att.astype(BF16))
            upd = _dot(v_hg_b[:, lanes], d["hg_khat"][:, lanes], TN)
            u_hg += [upd[0:128, 0:128], upd[128:256, 128:256]]
        zs, qd, zts = d["g"]
        v_g_b = v_ref[rows, 512:1024]
        att_g, u_g = [], []
        for p in range(2):
            lanes = slice(128 * p, 128 * (p + 1))
            att = _dot(qd[:, lanes], zts[N_LEVELS][p], NT) * attmask_ref[N_LEVELS]
            for lvl in range(N_LEVELS):
                att = att + _dot(zs[lvl][:, lanes], zts[lvl][p], NT) * attmask_ref[lvl]
            att_g.append(att.astype(BF16))
            u_g.append(packed_own(_dot(v_g_b[:, 256 * p:256 * (p + 1)], d["g_khat"][:, lanes], TN)))
        q_m_b = p_ref[rows, C_MQ:C_MQ + 256].astype(BF16)
        k_m_b = p_ref[rows, C_MK:C_MK + 256].astype(BF16)
        v_m_b = v_ref[rows, 1024:1536]
        khat_m_b = d["khat_m"].astype(BF16)
        s_m, u_m, rs_m = [], [], []
        for p in range(2):
            lanes = slice(128 * p, 128 * (p + 1))
            s_loc = _dot(q_m_b[:, lanes], halves_blockdiag(k_m_b[:, lanes]), NT) * d["w_loc"][:, lanes]
            s_m.append(s_loc.astype(BF16))
            u_m.append(packed_own(_dot(v_m_b[:, 256 * p:256 * (p + 1)], khat_m_b[:, lanes], TN)))
            rs_m += [jnp.sum(s_loc[:, 0:64], axis=-1, keepdims=True), jnp.sum(s_loc[:, 64:128], axis=-1, keepdims=True)]
        d.update(att_hg=att_hg, u_hg=u_hg, att_g=att_g, u_g=u_g, s_m=s_m, u_m=u_m, rs_m=rs_m,
                 q_m_b=q_m_b, v_hg_b=v_hg_b, v_g_b=v_g_b, v_m_b=v_m_b)

    def stage_values(d):
        d["o_hg"] = [_dot(d["att_hg"][p], pair_blockdiag(d["v_hg_b"][:, 256 * p:256 * (p + 1)])) for p in range(2)]
        d["o_g"] = [_dot(d["att_g"][p], pair_blockdiag(d["v_g_b"][:, 256 * p:256 * (p + 1)])) for p in range(2)]
        d["o_m"] = [_dot(d["s_m"][p], pair_blockdiag(d["v_m_b"][:, 256 * p:256 * (p + 1)])) for p in range(2)]

    def stage_state(d):
        rows = d["rows"]
        xlast = d["xlast"]
        o_heads = []
        for hh in range(4):
            hl = slice(128 * hh, 128 * (hh + 1))
            st = sthg_ref[hh]
            o_heads.append(d["o_hg"][hh // 2][:, 128 * (hh % 2):128 * (hh % 2 + 1)]
                           + _dot(d["hg_q0"][:, hl], st.T.astype(BF16)))
            sthg_ref[hh] = st * xlast[:, hl] + d["u_hg"][hh]
        y_ref[rows, 0:512] = (_head_norm(jnp.concatenate(o_heads, axis=1)) * p_ref[rows, C_HGZ:C_HGZ + 512]).astype(BF16)
        o_pairs = []
        for p in range(2):
            lanes = slice(128 * p, 128 * (p + 1))
            st = stg_ref[p]
            o_pairs.append(d["o_g"][p] + _dot(d["g_q0"][:, lanes], packed_state_operand(st), NT))
            stg_ref[p] = st * xlast[:, 512 + 128 * p:512 + 128 * (p + 1)] + d["u_g"][p]
        y_ref[rows, 512:1024] = (_head_norm(jnp.concatenate(o_pairs, axis=1)) * p_ref[rows, C_GZ:C_GZ + 512]).astype(BF16)
        b_exp, m_loc = d["b_exp"], d["m_loc"]
        m_prev = nm_ref[1:2, :]
        n_prev = nm_ref[0:1, :]
        m_t = jnp.maximum(b_exp + m_prev, m_loc)
        corr = jnp.exp(m_loc - m_t)
        inter = jnp.exp(b_exp + m_prev - m_t)
        m_new = m_t[CHUNK - 1:CHUNK, :]
        scale = jnp.exp(d["b_last"] + m_prev - m_new)
        upd_scale = jnp.exp(d["mk"] - m_new)
        qn = p_ref[rows, C_MQ:C_MQ + 256] * n_prev
        hs = []
        for p in range(2):
            lanes = slice(128 * p, 128 * (p + 1))
            ctm = ct_ref[p]
            num_inter = _dot(d["q_m_b"][:, lanes], packed_state_operand(ctm), NT)
            ct_ref[p] = ctm * scale[:, lanes] + d["u_m"][p] * upd_scale[:, lanes]
            for j in range(2):
                hh = 2 * p + j
                col = slice(64 * hh, 64 * hh + 1)
                den = (inter[:, col] * jnp.sum(qn[:, 64 * hh:64 * (hh + 1)], axis=-1, keepdims=True)
                       + corr[:, col] * d["rs_m"][hh])
                vs = slice(128 * j, 128 * (j + 1))
                num = inter[:, col] * num_inter[:, vs] + corr[:, col] * d["o_m"][p][:, vs]
                hs.append(num / jnp.maximum(jnp.abs(den), jnp.exp(-m_t[:, col])))
        h_m = jnp.concatenate(hs, axis=1) * p_ref[rows, C_MO:C_MO + 512]
        y_ref[rows, 1024:1536] = (_head_norm(h_m) * p_ref[rows, C_MZ:C_MZ + 512]).astype(BF16)
        nm_ref[0:1, :] = n_prev * scale + jnp.sum(d["khat_m"], axis=0, keepdims=True) * upd_scale
        nm_ref[1:2, :] = m_new

    chunks = [stage_factors(c) for c in range(n_chunks)]
    project_rest()
    for d in chunks:
        stage_mlstm_factors(d)
    for i in range(n_chunks + 2):
        if i < n_chunks:
            stage_scores(chunks[i])
        if 0 <= i - 1 < n_chunks:
            stage_values(chunks[i - 1])
        if 0 <= i - 2 < n_chunks:
            stage_state(chunks[i - 2])

    out = x + _dot(y_ref[...], wout_ref[...])
    if apply_final_norm:
        out = out * lax.rsqrt(jnp.mean(out * out, axis=-1, keepdims=True) + NORM_EPS) * fnw_ref[...]
    out_ref[0] = out


def _whole(arr):
    nd = arr.ndim
    return pl.BlockSpec(arr.shape, lambda b, t: (0,) * nd, pipeline_mode=pl.Buffered(1))


def _slab(arr, layer):
    nd = arr.ndim - 1
    return pl.BlockSpec((None,) + arr.shape[1:], lambda b, t: (layer,) + (0,) * nd, pipeline_mode=pl.Buffered(1))


def _layer_call(layer, apply_final_norm, x, stacked, shared, consts):
    bsz, seq, d = x.shape
    tb = TIME_BLOCK
    assert seq % tb == 0 and tb % CHUNK == 0 and d == D_MODEL
    const_arrays = [consts[k] for k in CONST_NAMES]
    in_specs = ([pl.BlockSpec((1, tb, d), lambda b, t: (b, t, 0))]
                + [_slab(a, layer) for a in stacked] + [_whole(a) for a in shared] + [_whole(a) for a in const_arrays])
    scratch = [
        pltpu.VMEM((tb, N_PROJ), F32),
        pltpu.VMEM((tb, 1536), BF16),
        pltpu.VMEM((tb, 768), F32),
        pltpu.VMEM((tb, 768), F32),
        pltpu.VMEM((tb + 8, 512), F32),
        pltpu.VMEM((tb, 1536), BF16),
        pltpu.VMEM((4, 128, 128), F32),
        pltpu.VMEM((2, 128, 128), F32),
        pltpu.VMEM((2, 128, 128), F32),
        pltpu.VMEM((8, 256), F32),
    ]
    return pl.pallas_call(
        functools.partial(_layer_kernel, layer, apply_final_norm),
        grid=(bsz, seq // tb),
        in_specs=in_specs,
        out_specs=pl.BlockSpec((1, tb, d), lambda b, t: (b, t, 0)),
        out_shape=jax.ShapeDtypeStruct(x.shape, F32),
        scratch_shapes=scratch,
        compiler_params=pltpu.CompilerParams(dimension_semantics=("arbitrary", "arbitrary"),
                                             vmem_limit_bytes=VMEM_LIMIT_BYTES),
        name=f"trunk_layer{layer}",
    )(x, *stacked, *shared, *const_arrays)


SEGMENT_MOVES = (
    (0, 0, 3072), (3088, 3072, 1536), (4632, 4608, 1024), (3072, C_SMALL + S_GLA_A, 16), (4624, C_SMALL + S_ML_I, 8))


TRANSPOSE_TILE = 256


def _reorder_bias(b):
    pad = jnp.zeros(b.shape[:-1] + (N_PROJ - (C_SMALL + S_ML_F + 4),), b.dtype)
    return jnp.concatenate([b[..., src:src + width] for src, _, width in SEGMENT_MOVES] + [pad], axis=-1)


def _weight_prep_kernel(wt_ref, o_ref):
    d_model = wt_ref.shape[1]
    for src, dst, width in SEGMENT_MOVES:
        for off in range(0, width, TRANSPOSE_TILE):
            n = min(TRANSPOSE_TILE, width - off)
            rows = wt_ref[src + off:src + off + n, :]
            if n < 128:
                rows = jnp.concatenate([rows, jnp.zeros((128 - n, d_model), rows.dtype)], axis=0)
            o_ref[:, dst + off:dst + off + n] = rows.T[:, 0:n].astype(o_ref.dtype)
    used = C_SMALL + S_ML_F + 4
    o_ref[:, used:N_PROJ] = jnp.zeros((d_model, N_PROJ - used), o_ref.dtype)


def _prepare_w_in(w_in):
    depth, d_model, n_in = w_in.shape
    return pl.pallas_call(
        _weight_prep_kernel,
        grid=(depth,),
        in_specs=[pl.BlockSpec((None, n_in, d_model), lambda l: (l, 0, 0), pipeline_mode=pl.Buffered(1))],
        out_specs=pl.BlockSpec((None, d_model, N_PROJ), lambda l: (l, 0, 0)),
        out_shape=jax.ShapeDtypeStruct((depth, d_model, N_PROJ), BF16),
        compiler_params=pltpu.CompilerParams(dimension_semantics=("arbitrary",), vmem_limit_bytes=VMEM_LIMIT_BYTES),
        name="prepare_w_in",
    )(jnp.transpose(w_in, (0, 2, 1)))


def kernel(x, norm_w, w_in, b_in, hg_lb_logits, hg_norm_w, gla_w_a2, gla_b_a2, gla_norm_w,
           ml_conv_w, ml_conv_b, ml_norm_w, w_out, final_norm_w):
    depth = w_in.shape[0]
    consts = _constants()
    wa2_pad = jnp.zeros((depth, 128, 256), F32).at[:, S_GLA_A:S_GLA_A + 16, :].set(gla_w_a2).astype(BF16)
    row = lambda v: v[:, None, :]
    stacked = (row(norm_w), _prepare_w_in(w_in), row(_reorder_bias(b_in)), row(hg_norm_w),
               wa2_pad, row(gla_b_a2), row(gla_norm_w), ml_conv_w, row(ml_conv_b), row(ml_norm_w),
               w_out.astype(BF16))
    shared = (hg_lb_logits, final_norm_w[None, :])
    for l in range(depth):
        x = _layer_call(l, l == depth - 1, x, stacked, shared, consts)
    return x
```
